```python
import jax, jax.numpy as jnp
from jax import lax
import numpy as np

D_MODEL = 1024
BATCH = 8
SEQ = 2048
DEPTH = 2

MEM_LEN = 256
HEAD_DIM = 64
SB_HEADS = 8
FOX_HEADS = 8
MEM_HEADS = 4
SB_WIDTH = SB_HEADS * HEAD_DIM
FOX_WIDTH = FOX_HEADS * HEAD_DIM
MEM_WIDTH = MEM_HEADS * HEAD_DIM
MIX_WIDTH = SB_WIDTH + FOX_WIDTH + MEM_WIDTH
TOTAL_HEADS = SB_HEADS + FOX_HEADS + MEM_HEADS
IN_WIDTH = 3 * SB_WIDTH + 3 * FOX_WIDTH + FOX_HEADS + MEM_WIDTH + MIX_WIDTH
Q_BLOCK = 128
EPS = 1e-6

kernel_name = "hybrid_stickbreak_fox_memxattn"


def rmsnorm(x, g):
    xf = x.astype(jnp.float32)
    y = xf * lax.rsqrt(jnp.mean(xf * xf, axis=-1, keepdims=True) + EPS)
    return (y * g.astype(jnp.float32)).astype(x.dtype)


def split_heads(t, n_heads):
    b, s, _ = t.shape
    return t.reshape(b, s, n_heads, HEAD_DIM).transpose(0, 2, 1, 3)


def merge_heads(t):
    b, h, s, d = t.shape
    return t.transpose(0, 2, 1, 3).reshape(b, s, h * d)


def stick_breaking_attention(q, k, v):
    seq = q.shape[2]
    scale = HEAD_DIM ** -0.5
    outs = []
    for i in range(seq // Q_BLOCK):
        start, end = i * Q_BLOCK, (i + 1) * Q_BLOCK
        qb, kb, vb = q[:, :, start:end], k[:, :, :end], v[:, :, :end]
        z = jnp.einsum('bhqd,bhkd->bhqk', qb, kb,
                       preferred_element_type=jnp.float32) * scale
        q_pos = start + jnp.arange(Q_BLOCK)[:, None]
        k_pos = jnp.arange(end)[None, :]
        strict = k_pos < q_pos
        log_fail = jnp.where(strict, -jax.nn.softplus(z), 0.0)
        suffix = lax.cumsum(log_fail, axis=3, reverse=True) - log_fail
        log_a = jax.nn.log_sigmoid(z) + suffix
        a = jnp.where(strict, jnp.exp(log_a), 0.0)
        outs.append(jnp.einsum('bhqk,bhkd->bhqd', a.astype(vb.dtype), vb))
    return jnp.concatenate(outs, axis=2)


def forgetting_attention(q, k, v, log_f_cum):
    seq = q.shape[2]
    scale = HEAD_DIM ** -0.5
    outs = []
    for i in range(seq // Q_BLOCK):
        start, end = i * Q_BLOCK, (i + 1) * Q_BLOCK
        qb, kb, vb = q[:, :, start:end], k[:, :, :end], v[:, :, :end]
        logits = jnp.einsum('bhqd,bhkd->bhqk', qb, kb,
                            preferred_element_type=jnp.float32) * scale
        logits = logits + log_f_cum[:, :, start:end, None] - log_f_cum[:, :, None, :end]
        q_pos = start + jnp.arange(Q_BLOCK)[:, None]
        k_pos = jnp.arange(end)[None, :]
        logits = jnp.where(k_pos <= q_pos, logits, -jnp.inf)
        p = jax.nn.softmax(logits, axis=-1)
        outs.append(jnp.einsum('bhqk,bhkd->bhqd', p.astype(vb.dtype), vb))
    return jnp.concatenate(outs, axis=2)


def memory_attention(q, k, v):
    logits = jnp.einsum('bhqd,bhkd->bhqk', q, k,
                        preferred_element_type=jnp.float32) * (HEAD_DIM ** -0.5)
    p = jax.nn.softmax(logits, axis=-1)
    return jnp.einsum('bhqk,bhkd->bhqd', p.astype(v.dtype), v)


def setup_inputs(seed: int = 0) -> dict:
    key = jax.random.key(seed)
    ks = jax.random.split(key, 11)
    f32 = jnp.float32
    x = jax.random.normal(ks[0], (BATCH, SEQ, D_MODEL), f32)
    mem = jax.random.normal(ks[1], (BATCH, MEM_LEN, D_MODEL), f32)
    norm_w = 1.0 + 0.02 * jax.random.normal(ks[2], (DEPTH, D_MODEL), f32)
    w_in = jax.random.normal(ks[3], (DEPTH, D_MODEL, IN_WIDTH), f32) * D_MODEL ** -0.5
    b_forget = 0.1 * jax.random.normal(ks[4], (DEPTH, FOX_HEADS), f32)
    mem_norm_w = 1.0 + 0.02 * jax.random.normal(ks[5], (DEPTH, D_MODEL), f32)
    w_mem_kv = jax.random.normal(ks[6], (DEPTH, D_MODEL, 2 * MEM_WIDTH), f32) * D_MODEL ** -0.5
    out_norm_w = 1.0 + 0.02 * jax.random.normal(ks[7], (DEPTH, MIX_WIDTH), f32)
    w_out = jax.random.normal(ks[8], (DEPTH, MIX_WIDTH, D_MODEL), f32) * MIX_WIDTH ** -0.5
    final_norm_w = 1.0 + 0.02 * jax.random.normal(ks[9], (D_MODEL,), f32)
    return {"x": x, "mem": mem, "norm_w": norm_w, "w_in": w_in, "b_forget": b_forget,
            "mem_norm_w": mem_norm_w, "w_mem_kv": w_mem_kv, "out_norm_w": out_norm_w,
            "w_out": w_out, "final_norm_w": final_norm_w}


def reference(x, mem, norm_w, w_in, b_forget, mem_norm_w, w_mem_kv, out_norm_w, w_out,
              final_norm_w):
    sizes = [SB_WIDTH] * 3 + [FOX_WIDTH] * 3 + [FOX_HEADS, MEM_WIDTH, MIX_WIDTH]
    offsets = [int(o) for o in np.cumsum(sizes)[:-1]]
    b, s, _ = x.shape
    for layer in range(DEPTH):
        h = rmsnorm(x, norm_w[layer])
        proj = h @ w_in[layer]
        (sb_q, sb_k, sb_v, fx_q, fx_k, fx_v, f_logit, m_q, gate) = jnp.split(proj, offsets, axis=-1)

        sb_out = stick_breaking_attention(split_heads(sb_q, SB_HEADS), split_heads(sb_k, SB_HEADS),
                                          split_heads(sb_v, SB_HEADS))

        log_f = jax.nn.log_sigmoid((f_logit + b_forget[layer]).astype(jnp.float32))
        log_f_cum = jnp.cumsum(log_f.transpose(0, 2, 1), axis=-1)
        fx_out = forgetting_attention(split_heads(fx_q, FOX_HEADS), split_heads(fx_k, FOX_HEADS),
                                      split_heads(fx_v, FOX_HEADS), log_f_cum)

        mem_kv = rmsnorm(mem, mem_norm_w[layer]) @ w_mem_kv[layer]
        m_k, m_v = jnp.split(mem_kv, 2, axis=-1)
        m_out = memory_attention(split_heads(m_q, MEM_HEADS), split_heads(m_k, MEM_HEADS),
                                 split_heads(m_v, MEM_HEADS))

        y = jnp.concatenate([merge_heads(sb_out), merge_heads(fx_out), merge_heads(m_out)], axis=-1)
        yf = y.astype(jnp.float32).reshape(b, s, TOTAL_HEADS, HEAD_DIM)
        yf = yf * lax.rsqrt(jnp.mean(yf * yf, axis=-1, keepdims=True) + EPS)
        yf = yf.reshape(b, s, MIX_WIDTH) * out_norm_w[layer].astype(jnp.float32)
        y = (yf * jax.nn.silu(gate.astype(jnp.float32))).astype(x.dtype)
        x = x + y @ w_out[layer]
    return rmsnorm(x, final_norm_w)
```

```python
import functools

import jax
import jax.numpy as jnp
from jax import lax
from jax.experimental import pallas as pl
from jax.experimental.pallas import tpu as pltpu

HEAD_DIM = 64
SB_HEADS = 8
FOX_HEADS = 8
MEM_HEADS = 4
SB_WIDTH = SB_HEADS * HEAD_DIM
FOX_WIDTH = FOX_HEADS * HEAD_DIM
MEM_WIDTH = MEM_HEADS * HEAD_DIM
MIX_WIDTH = SB_WIDTH + FOX_WIDTH + MEM_WIDTH
QKV_WIDTH = 3 * SB_WIDTH + 3 * FOX_WIDTH
EPS = 1e-6
LOG2E = 1.4426950408889634
Q_SCALE = HEAD_DIM ** -0.5 * LOG2E
MASKED = -1e30

LANES = 128
ROW_TILE = 512
COL_CHUNK = 512
ATT_TILE = 256
MEM_Q_TILE = 512
CUM_CHUNK = 128
VMEM_LIMIT = 56 * 1024 * 1024

F32 = jnp.float32
BF16 = jnp.bfloat16
NT_DIMS = (((1,), (1,)), ((), ()))


def _params(*semantics):
    return pltpu.CompilerParams(dimension_semantics=semantics, vmem_limit_bytes=VMEM_LIMIT)


def _rmsnorm(x, w):
    return x * lax.rsqrt(jnp.mean(x * x, axis=-1, keepdims=True) + EPS) * w


def _dot(a, b):
    return jnp.dot(a, b, preferred_element_type=F32)


def _dot_nt(a, b):
    return lax.dot_general(a, b, NT_DIMS, preferred_element_type=F32)


def _inproj_kernel(x_ref, nw_ref, wqkv_ref, wmq_ref, wg_ref, wf_ref, cs_ref,
                   qkv_ref, mq_ref, gate_ref, f_ref):
    hb = _rmsnorm(x_ref[...], nw_ref[...]).astype(BF16)
    for c in range(QKV_WIDTH // COL_CHUNK):
        sl = slice(c * COL_CHUNK, (c + 1) * COL_CHUNK)
        qkv_ref[:, sl] = (_dot(hb, wqkv_ref[:, sl]) * cs_ref[:, sl]).astype(BF16)
    mq_ref[...] = (_dot(hb, wmq_ref[...]) * Q_SCALE).astype(BF16)
    for c in range(0, MIX_WIDTH, 256):
        sl = slice(c, c + 256)
        gate_ref[:, sl] = _dot(hb, wg_ref[:, sl])
    f_ref[...] = _dot(hb, wf_ref[...])


def _inproj(x2d, norm_w, wqkv, wmq, wg, wf, col_scale):
    m, d = x2d.shape
    full = lambda shape: pl.BlockSpec(shape, lambda i: (0, 0))
    rows = lambda n: pl.BlockSpec((ROW_TILE, n), lambda i: (i, 0))
    return pl.pallas_call(
        _inproj_kernel,
        grid=(m // ROW_TILE,),
        in_specs=[rows(d), full((1, d)), full(wqkv.shape), full(wmq.shape), full(wg.shape),
                  full(wf.shape), full((1, QKV_WIDTH))],
        out_specs=[rows(QKV_WIDTH), rows(MEM_WIDTH), rows(MIX_WIDTH), rows(LANES)],
        out_shape=[jax.ShapeDtypeStruct((m, QKV_WIDTH), BF16),
                   jax.ShapeDtypeStruct((m, MEM_WIDTH), BF16),
                   jax.ShapeDtypeStruct((m, MIX_WIDTH), F32),
                   jax.ShapeDtypeStruct((m, LANES), F32)],
        compiler_params=_params("parallel"),
        name="inproj",
    )(x2d, norm_w, wqkv, wmq, wg, wf, col_scale)


def _memkv_kernel(m_ref, nw_ref, w_ref, o_ref):
    hb = _rmsnorm(m_ref[...], nw_ref[...]).astype(BF16)
    o_ref[...] = _dot(hb, w_ref[...]).astype(BF16)


def _memkv(mem2d, norm_w, w):
    m, d = mem2d.shape
    n = w.shape[1]
    return pl.pallas_call(
        _memkv_kernel,
        grid=(m // ROW_TILE,),
        in_specs=[pl.BlockSpec((ROW_TILE, d), lambda i: (i, 0)),
                  pl.BlockSpec((1, d), lambda i: (0, 0)),
                  pl.BlockSpec((d, n), lambda i: (0, 0))],
        out_specs=pl.BlockSpec((ROW_TILE, n), lambda i: (i, 0)),
        out_shape=jax.ShapeDtypeStruct((m, n), BF16),
        compiler_params=_params("parallel"),
        name="memkv",
    )(mem2d, norm_w, w)


def _outproj_kernel(ysb_ref, yfx_ref, ym_ref, x_ref, w_ref, fnw_ref, o_ref, *, final):
    acc = x_ref[...]
    acc = acc + _dot(ysb_ref[...], w_ref[0:SB_WIDTH, :])
    acc = acc + _dot(yfx_ref[...], w_ref[SB_WIDTH:SB_WIDTH + FOX_WIDTH, :])
    acc = acc + _dot(ym_ref[...], w_ref[SB_WIDTH + FOX_WIDTH:MIX_WIDTH, :])
    if final:
        acc = _rmsnorm(acc, fnw_ref[...])
    o_ref[...] = acc


def _outproj(ysb, yfx, ym, x2d, w, final_norm_w, final):
    m, d = x2d.shape
    rows = lambda n: pl.BlockSpec((ROW_TILE, n), lambda i: (i, 0))
    return pl.pallas_call(
        functools.partial(_outproj_kernel, final=final),
        grid=(m // ROW_TILE,),
        in_specs=[rows(SB_WIDTH), rows(FOX_WIDTH), rows(MEM_WIDTH), rows(d),
                  pl.BlockSpec(w.shape, lambda i: (0, 0)),
                  pl.BlockSpec((1, d), lambda i: (0, 0))],
        out_specs=rows(d),
        out_shape=jax.ShapeDtypeStruct((m, d), F32),
        compiler_params=_params("parallel"),
        name="outproj",
    )(ysb, yfx, ym, x2d, w, final_norm_w)


def _split3(x):
    hi = x.astype(BF16)
    r = x - hi.astype(F32)
    mid = r.astype(BF16)
    lo = (r - mid.astype(F32)).astype(BF16)
    return hi, mid, lo


def _fgate_kernel(f_ref, b_ref, o_ref):
    t = f_ref[...] + b_ref[...]
    lf = (jnp.minimum(t, 0.0) - jnp.log(1.0 + jnp.exp(-jnp.abs(t)))) * LOG2E
    n = lf.shape[0]
    r = lax.broadcasted_iota(jnp.int32, (CUM_CHUNK, CUM_CHUNK), 0)
    c = lax.broadcasted_iota(jnp.int32, (CUM_CHUNK, CUM_CHUNK), 1)
    upper = (r <= c).astype(BF16)
    local = sum(_dot(p, upper) for p in _split3(lf))
    totals = jnp.broadcast_to(local[:, CUM_CHUNK - 1:CUM_CHUNK], (n, CUM_CHUNK))
    chunks = n // FOX_HEADS
    gr = lax.broadcasted_iota(jnp.int32, (n, n), 0)
    gc = lax.broadcasted_iota(jnp.int32, (n, n), 1)
    earlier = ((gr // chunks == gc // chunks) & (gc < gr)).astype(BF16)
    offs = sum(_dot(earlier, p) for p in _split3(totals))
    o_ref[...] = local + offs


def _fgate(f_rows, bias_rows, batch):
    n = f_rows.shape[0] // batch
    return pl.pallas_call(
        _fgate_kernel,
        grid=(batch,),
        in_specs=[pl.BlockSpec((n, CUM_CHUNK), lambda b: (b, 0)),
                  pl.BlockSpec((n, 1), lambda b: (0, 0))],
        out_specs=pl.BlockSpec((n, CUM_CHUNK), lambda b: (b, 0)),
        out_shape=jax.ShapeDtypeStruct(f_rows.shape, F32),
        compiler_params=_params("parallel"),
        name="fgate",
    )(f_rows, bias_rows)


def _lane_ids():
    return lax.broadcasted_iota(jnp.int32, (1, LANES), 1)


def _head_queries(q2, lane, hh):
    keep = (lane >= hh * HEAD_DIM) & (lane < (hh + 1) * HEAD_DIM)
    return jnp.where(keep, q2, jnp.zeros_like(q2))


def _finish(o, gate, onw, lane):
    first = lane < HEAD_DIM
    o2 = o * o
    s0 = jnp.sum(jnp.where(first, o2, 0.0), axis=-1, keepdims=True)
    s1 = jnp.sum(jnp.where(first, 0.0, o2), axis=-1, keepdims=True)
    ms = jnp.where(first, s0, s1) * (1.0 / HEAD_DIM)
    yn = o * lax.rsqrt(ms + EPS) * onw
    g = gate.astype(F32)
    return (yn * (g * jax.nn.sigmoid(g))).astype(BF16)


def _sb_kernel(q_ref, k_ref, v_ref, g_ref, onw_ref, y_ref):
    t = ATT_TILE
    qi = pl.program_id(2)
    q2 = q_ref[0]
    lane = _lane_ids()
    row = lax.broadcasted_iota(jnp.int32, (t, t), 0)
    col = lax.broadcasted_iota(jnp.int32, (t, t), 1)
    strict = col < row
    incl = (row >= col).astype(BF16)

    def block(qh, j, carry, acc, diag):
        start = pl.multiple_of(j * t, t)
        kb = k_ref[0, pl.ds(start, t), :]
        vb = v_ref[0, pl.ds(start, t), :]
        z = _dot_nt(qh, kb)
        sp = jnp.maximum(z, 0.0) + jnp.log(1.0 + jnp.exp2(-jnp.abs(z))) * LOG2E
        if diag:
            sp = jnp.where(strict, sp, 0.0)
        hi = sp.astype(BF16)
        lo = (sp - hi.astype(F32)).astype(BF16)
        c = _dot(hi, incl) + _dot(lo, incl)
        a = jnp.exp2(z - c + carry)
        if diag:
            a = jnp.where(strict, a, 0.0)
        acc = acc + _dot(a.astype(BF16), vb)
        carry = carry - c[:, 0:1]
        return carry, acc

    outs = []
    for hh in range(2):
        qh = _head_queries(q2, lane, hh)
        carry, acc = block(qh, qi, jnp.zeros((t, 1), F32), jnp.zeros((t, LANES), F32), True)

        def body(jj, st, qh=qh):
            return block(qh, qi - 1 - jj, st[0], st[1], False)

        carry, acc = lax.fori_loop(0, qi, body, (carry, acc))
        outs.append(acc)
    o = jnp.where(lane < HEAD_DIM, outs[0], outs[1])
    y_ref[0] = _finish(o, g_ref[0], onw_ref[...], lane)


def _fox_kernel(q_ref, k_ref, v_ref, lq_ref, lk_ref, g_ref, onw_ref, y_ref):
    t = ATT_TILE
    hp = pl.program_id(1)
    qi = pl.program_id(2)
    q2 = q_ref[0]
    lqt = lq_ref[0]
    lane = _lane_ids()
    head_ids = lax.broadcasted_iota(jnp.int32, (1, FOX_HEADS), 1)
    row = lax.broadcasted_iota(jnp.int32, (t, t), 0)
    col = lax.broadcasted_iota(jnp.int32, (t, t), 1)
    causal = col <= row

    outs = []
    for hh in range(2):
        h = 2 * hp + hh
        qh = _head_queries(q2, lane, hh)
        lq = jnp.sum(jnp.where(head_ids == h, lqt, 0.0), axis=-1, keepdims=True)

        def scores(j, qh=qh, lq=lq, h=h):
            start = pl.multiple_of(j * t, t)
            kb = k_ref[0, pl.ds(start, t), :]
            vb = v_ref[0, pl.ds(start, t), :]
            lk = lk_ref[0, h, pl.ds(j, 1), :]
            return _dot_nt(qh, kb) + lq - lk, vb

        s, vb = scores(qi)
        s = jnp.where(causal, s, MASKED)
        m = jnp.max(s, axis=-1, keepdims=True)
        p = jnp.exp2(s - m)
        l = jnp.sum(p, axis=-1, keepdims=True)
        acc = _dot(p.astype(BF16), vb)

        def body(jj, st, scores=scores):
            m, l, acc = st
            s, vb = scores(qi - 1 - jj)
            m_new = jnp.maximum(m, jnp.max(s, axis=-1, keepdims=True))
            alpha = jnp.exp2(m - m_new)
            p = jnp.exp2(s - m_new)
            l = alpha * l + jnp.sum(p, axis=-1, keepdims=True)
            acc = alpha * acc + _dot(p.astype(BF16), vb)
            return m_new, l, acc

        m, l, acc = lax.fori_loop(0, qi, body, (m, l, acc))
        outs.append(acc / l)
    o = jnp.where(lane < HEAD_DIM, outs[0], outs[1])
    y_ref[0] = _finish(o, g_ref[0], onw_ref[...], lane)


def _mem_kernel(q_ref, k_ref, v_ref, g_ref, onw_ref, y_ref):
    q2 = q_ref[0]
    k2 = k_ref[0]
    v2 = v_ref[0]
    lane = _lane_ids()
    outs = []
    for hh in range(2):
        s = _dot_nt(_head_queries(q2, lane, hh), k2)
        p = jnp.exp2(s - jnp.max(s, axis=-1, keepdims=True))
        l = jnp.sum(p, axis=-1, keepdims=True)
        outs.append(_dot(p.astype(BF16), v2) / l)
    o = jnp.where(lane < HEAD_DIM, outs[0], outs[1])
    y_ref[0] = _finish(o, g_ref[0], onw_ref[...], lane)


def _att_call(kernel, name, batch, seq, q_tile, width, in_arrays, in_specs):
    grid = (batch, width // LANES, seq // q_tile)
    return pl.pallas_call(
        kernel,
        grid=grid,
        in_specs=in_specs,
        out_specs=pl.BlockSpec((1, q_tile, LANES), lambda b, hp, i: (b, i, hp)),
        out_shape=jax.ShapeDtypeStruct((batch, seq, width), BF16),
        compiler_params=_params("parallel", "parallel", "arbitrary"),
        name=name,
    )(*in_arrays)


def _lane_block(rows, first_block):
    if rows is None:
        return lambda full: pl.BlockSpec((1, full, LANES), lambda b, hp, i: (b, 0, first_block + hp))
    return pl.BlockSpec((1, rows, LANES), lambda b, hp, i: (b, i, first_block + hp))


def _onw_block(first_block):
    return pl.BlockSpec((1, LANES), lambda b, hp, i: (0, first_block + hp))


def kernel(x, mem, norm_w, w_in, b_forget, mem_norm_w, w_mem_kv, out_norm_w, w_out, final_norm_w):
    batch, seq, d = x.shape
    mem_len = mem.shape[1]
    depth = w_in.shape[0]
    m = batch * seq
    nblk = seq // ATT_TILE
    chunks = seq // CUM_CHUNK
    blocks = lambda width: width // LANES
    sb_q, sb_k, sb_v = 0, blocks(SB_WIDTH), 2 * blocks(SB_WIDTH)
    fx_q = 3 * blocks(SB_WIDTH)
    fx_k, fx_v = fx_q + blocks(FOX_WIDTH), fx_q + 2 * blocks(FOX_WIDTH)
    g_sb, g_fx, g_m = 0, blocks(SB_WIDTH), blocks(SB_WIDTH + FOX_WIDTH)

    ones = jnp.ones((SB_WIDTH,), F32)
    col_scale = jnp.concatenate([ones * Q_SCALE, ones, ones, ones * Q_SCALE, ones, ones])[None, :]
    off_f = QKV_WIDTH
    off_mq = off_f + FOX_HEADS
    off_g = off_mq + MEM_WIDTH

    x2d = x.reshape(m, d)
    mem2d = mem.reshape(batch * mem_len, d)
    for layer in range(depth):
        w = w_in[layer]
        wqkv = w[:, :QKV_WIDTH].astype(BF16)
        wf = jnp.pad(w[:, off_f:off_mq], ((0, 0), (0, LANES - FOX_HEADS))).astype(BF16)
        wmq = w[:, off_mq:off_g].astype(BF16)
        wg = w[:, off_g:].astype(BF16)
        qkv, mq, gate, f = _inproj(x2d, norm_w[layer][None, :], wqkv, wmq, wg, wf, col_scale)
        qkv = qkv.reshape(batch, seq, QKV_WIDTH)
        mq = mq.reshape(batch, seq, MEM_WIDTH)
        gate = gate.reshape(batch, seq, MIX_WIDTH)
        onw = out_norm_w[layer][None, :]

        f_rows = f[:, :FOX_HEADS].reshape(batch, seq, FOX_HEADS).transpose(0, 2, 1)
        f_rows = f_rows.reshape(batch * FOX_HEADS * chunks, CUM_CHUNK)
        bias_rows = jnp.repeat(b_forget[layer], chunks)[:, None]
        lfc = _fgate(f_rows, bias_rows, batch).reshape(batch, FOX_HEADS, seq)
        lk = lfc.reshape(batch, FOX_HEADS, nblk, ATT_TILE)
        lq = lfc.transpose(0, 2, 1)

        mem_kv = _memkv(mem2d, mem_norm_w[layer][None, :], w_mem_kv[layer].astype(BF16))
        mem_kv = mem_kv.reshape(batch, mem_len, 2 * MEM_WIDTH)

        y_sb = _att_call(
            _sb_kernel, "sb_attention", batch, seq, ATT_TILE, SB_WIDTH,
            (qkv, qkv, qkv, gate, onw),
            [_lane_block(ATT_TILE, sb_q), _lane_block(None, sb_k)(seq), _lane_block(None, sb_v)(seq),
             _lane_block(ATT_TILE, g_sb), _onw_block(g_sb)])
        y_fx = _att_call(
            _fox_kernel, "fox_attention", batch, seq, ATT_TILE, FOX_WIDTH,
            (qkv, qkv, qkv, lq, lk, gate, onw),
            [_lane_block(ATT_TILE, fx_q), _lane_block(None, fx_k)(seq), _lane_block(None, fx_v)(seq),
             pl.BlockSpec((1, ATT_TILE, FOX_HEADS), lambda b, hp, i: (b, i, 0)),
             pl.BlockSpec((1, FOX_HEADS, nblk, ATT_TILE), lambda b, hp, i: (b, 0, 0, 0)),
             _lane_block(ATT_TILE, g_fx), _onw_block(g_fx)])
        y_m = _att_call(
            _mem_kernel, "mem_attention", batch, seq, MEM_Q_TILE, MEM_WIDTH,
            (mq, mem_kv, mem_kv, gate, onw),
            [_lane_block(MEM_Q_TILE, 0), _lane_block(None, 0)(mem_len),
             _lane_block(None, blocks(MEM_WIDTH))(mem_len),
             _lane_block(MEM_Q_TILE, g_m), _onw_block(g_m)])

        x2d = _outproj(y_sb.reshape(m, SB_WIDTH), y_fx.reshape(m, FOX_WIDTH),
                       y_m.reshape(m, MEM_WIDTH), x2d, w_out[layer].astype(BF16),
                       final_norm_w[None, :], layer == depth - 1)
    return x2d.reshape(batch, seq, d)
```

```python
import functools

import jax
import jax.numpy as jnp
from jax import lax
from jax.experimental import pallas as pl
from jax.experimental.pallas import tpu as pltpu

HEAD_DIM = 64
SB_HEADS = 8
FOX_HEADS = 8
MEM_HEADS = 4
SB_WIDTH = SB_HEADS * HEAD_DIM
FOX_WIDTH = FOX_HEADS * HEAD_DIM
MEM_WIDTH = MEM_HEADS * HEAD_DIM
MIX_WIDTH = SB_WIDTH + FOX_WIDTH + MEM_WIDTH
QKV_WIDTH = 3 * SB_WIDTH + 3 * FOX_WIDTH
EPS = 1e-6
LOG2E = 1.4426950408889634
Q_SCALE = HEAD_DIM ** -0.5 * LOG2E
MASKED = -1e30
ZERO_WEIGHT_LOG2 = -150.0
BOUND_SLACK_LOG2 = 2.0

LANES = 128
ROW_TILE = 512
COL_CHUNK = 512
ATT_TILE = 256
MEM_Q_TILE = 512
CUM_CHUNK = 128
VMEM_LIMIT = 56 * 1024 * 1024

F32 = jnp.float32
BF16 = jnp.bfloat16
NT_DIMS = (((1,), (1,)), ((), ()))


def _params(*semantics):
    return pltpu.CompilerParams(dimension_semantics=semantics, vmem_limit_bytes=VMEM_LIMIT)


def _rmsnorm(x, w):
    return x * lax.rsqrt(jnp.mean(x * x, axis=-1, keepdims=True) + EPS) * w


def _dot(a, b):
    return jnp.dot(a, b, preferred_element_type=F32)


def _dot_nt(a, b):
    return lax.dot_general(a, b, NT_DIMS, preferred_element_type=F32)


def _inproj_kernel(x_ref, nw_ref, wqkv_ref, wmq_ref, wg_ref, wf_ref, cs_ref,
                   qkv_ref, mq_ref, gate_ref, f_ref):
    hb = _rmsnorm(x_ref[...], nw_ref[...]).astype(BF16)
    for c in range(QKV_WIDTH // COL_CHUNK):
        sl = slice(c * COL_CHUNK, (c + 1) * COL_CHUNK)
        qkv_ref[:, sl] = (_dot(hb, wqkv_ref[:, sl]) * cs_ref[:, sl]).astype(BF16)
    mq_ref[...] = (_dot(hb, wmq_ref[...]) * Q_SCALE).astype(BF16)
    for c in range(0, MIX_WIDTH, 256):
        sl = slice(c, c + 256)
        gate_ref[:, sl] = _dot(hb, wg_ref[:, sl])
    f_ref[...] = _dot(hb, wf_ref[...])


def _inproj(x2d, norm_w, wqkv, wmq, wg, wf, col_scale):
    m, d = x2d.shape
    full = lambda shape: pl.BlockSpec(shape, lambda i: (0, 0))
    rows = lambda n: pl.BlockSpec((ROW_TILE, n), lambda i: (i, 0))
    return pl.pallas_call(
        _inproj_kernel,
        grid=(m // ROW_TILE,),
        in_specs=[rows(d), full((1, d)), full(wqkv.shape), full(wmq.shape), full(wg.shape),
                  full(wf.shape), full((1, QKV_WIDTH))],
        out_specs=[rows(QKV_WIDTH), rows(MEM_WIDTH), rows(MIX_WIDTH), rows(LANES)],
        out_shape=[jax.ShapeDtypeStruct((m, QKV_WIDTH), BF16),
                   jax.ShapeDtypeStruct((m, MEM_WIDTH), BF16),
                   jax.ShapeDtypeStruct((m, MIX_WIDTH), F32),
                   jax.ShapeDtypeStruct((m, LANES), F32)],
        compiler_params=_params("parallel"),
        name="inproj",
    )(x2d, norm_w, wqkv, wmq, wg, wf, col_scale)


def _memkv_kernel(m_ref, nw_ref, w_ref, o_ref):
    hb = _rmsnorm(m_ref[...], nw_ref[...]).astype(BF16)
    o_ref[...] = _dot(hb, w_ref[...]).astype(BF16)


def _memkv(mem2d, norm_w, w):
    m, d = mem2d.shape
    n = w.shape[1]
    return pl.pallas_call(
        _memkv_kernel,
        grid=(m // ROW_TILE,),
        in_specs=[pl.BlockSpec((ROW_TILE, d), lambda i: (i, 0)),
                  pl.BlockSpec((1, d), lambda i: (0, 0)),
                  pl.BlockSpec((d, n), lambda i: (0, 0))],
        out_specs=pl.BlockSpec((ROW_TILE, n), lambda i: (i, 0)),
        out_shape=jax.ShapeDtypeStruct((m, n), BF16),
        compiler_params=_params("parallel"),
        name="memkv",
    )(mem2d, norm_w, w)


def _outproj_kernel(ysb_ref, yfx_ref, ym_ref, x_ref, w_ref, fnw_ref, o_ref, *, final):
    acc = x_ref[...]
    acc = acc + _dot(ysb_ref[...], w_ref[0:SB_WIDTH, :])
    acc = acc + _dot(yfx_ref[...], w_ref[SB_WIDTH:SB_WIDTH + FOX_WIDTH, :])
    acc = acc + _dot(ym_ref[...], w_ref[SB_WIDTH + FOX_WIDTH:MIX_WIDTH, :])
    if final:
        acc = _rmsnorm(acc, fnw_ref[...])
    o_ref[...] = acc


def _outproj(ysb, yfx, ym, x2d, w, final_norm_w, final):
    m, d = x2d.shape
    rows = lambda n: pl.BlockSpec((ROW_TILE, n), lambda i: (i, 0))
    return pl.pallas_call(
        functools.partial(_outproj_kernel, final=final),
        grid=(m // ROW_TILE,),
        in_specs=[rows(SB_WIDTH), rows(FOX_WIDTH), rows(MEM_WIDTH), rows(d),
                  pl.BlockSpec(w.shape, lambda i: (0, 0)),
                  pl.BlockSpec((1, d), lambda i: (0, 0))],
        out_specs=rows(d),
        out_shape=jax.ShapeDtypeStruct((m, d), F32),
        compiler_params=_params("parallel"),
        name="outproj",
    )(ysb, yfx, ym, x2d, w, final_norm_w)


def _split3(x):
    hi = x.astype(BF16)
    r = x - hi.astype(F32)
    mid = r.astype(BF16)
    lo = (r - mid.astype(F32)).astype(BF16)
    return hi, mid, lo


def _fgate_kernel(f_ref, b_ref, o_ref):
    t = f_ref[...] + b_ref[...]
    lf = (jnp.minimum(t, 0.0) - jnp.log(1.0 + jnp.exp(-jnp.abs(t)))) * LOG2E
    n = lf.shape[0]
    r = lax.broadcasted_iota(jnp.int32, (CUM_CHUNK, CUM_CHUNK), 0)
    c = lax.broadcasted_iota(jnp.int32, (CUM_CHUNK, CUM_CHUNK), 1)
    upper = (r <= c).astype(BF16)
    local = sum(_dot(p, upper) for p in _split3(lf))
    totals = jnp.broadcast_to(local[:, CUM_CHUNK - 1:CUM_CHUNK], (n, CUM_CHUNK))
    chunks = n // FOX_HEADS
    gr = lax.broadcasted_iota(jnp.int32, (n, n), 0)
    gc = lax.broadcasted_iota(jnp.int32, (n, n), 1)
    earlier = ((gr // chunks == gc // chunks) & (gc < gr)).astype(BF16)
    offs = sum(_dot(earlier, p) for p in _split3(totals))
    o_ref[...] = local + offs


def _fgate(f_rows, bias_rows, batch):
    n = f_rows.shape[0] // batch
    return pl.pallas_call(
        _fgate_kernel,
        grid=(batch,),
        in_specs=[pl.BlockSpec((n, CUM_CHUNK), lambda b: (b, 0)),
                  pl.BlockSpec((n, 1), lambda b: (0, 0))],
        out_specs=pl.BlockSpec((n, CUM_CHUNK), lambda b: (b, 0)),
        out_shape=jax.ShapeDtypeStruct(f_rows.shape, F32),
        compiler_params=_params("parallel"),
        name="fgate",
    )(f_rows, bias_rows)


def _lane_ids():
    return lax.broadcasted_iota(jnp.int32, (1, LANES), 1)


def _head_lanes(lane, hh):
    return (lane >= hh * HEAD_DIM) & (lane < (hh + 1) * HEAD_DIM)


def _head_queries(q2, lane, hh):
    return jnp.where(_head_lanes(lane, hh), q2, jnp.zeros_like(q2))


def _rep(col):
    return jnp.broadcast_to(col, (col.shape[0], LANES))


def _wide(x):
    return jnp.concatenate([x] * (ATT_TILE // LANES), axis=1)


def _finish(o, gate, onw, lane):
    first = lane < HEAD_DIM
    o2 = o * o
    s0 = jnp.sum(jnp.where(first, o2, 0.0), axis=-1, keepdims=True)
    s1 = jnp.sum(jnp.where(first, 0.0, o2), axis=-1, keepdims=True)
    ms = jnp.where(first, s0, s1) * (1.0 / HEAD_DIM)
    yn = o * lax.rsqrt(ms + EPS) * onw
    g = gate.astype(F32)
    return (yn * (g * jax.nn.sigmoid(g))).astype(BF16)


def _block_masks():
    t = ATT_TILE
    row = lax.broadcasted_iota(jnp.int32, (t, t), 0)
    col = lax.broadcasted_iota(jnp.int32, (t, t), 1)
    return row, col


def _top_rows(fn, x, fresh_top):
    if not fresh_top:
        return x
    if x.shape[0] == ATT_TILE:
        return fn(x)
    return jnp.concatenate([fn(x[:ATT_TILE]), x[ATT_TILE:]], axis=0)


def _sb_kernel(q_ref, k_ref, v_ref, g_ref, onw_ref, y_ref, carry_ref, acc_ref, alive_ref):
    t = ATT_TILE
    nq = q_ref.shape[1] // t
    lane = _lane_ids()
    row, col = _block_masks()
    strict = col < row
    incl = (row >= col).astype(BF16)

    def tile(hh, row0, nrows, j, carry_in, fresh_top):
        qh = _head_queries(q_ref[0, pl.ds(row0, nrows), :], lane, hh)
        start = pl.multiple_of(j * t, t)
        kb = k_ref[0, pl.ds(start, t), :]
        vb = v_ref[0, pl.ds(start, t), :]
        z = _dot_nt(qh, kb)
        sp = jnp.maximum(z, 0.0) + jnp.log(1.0 + jnp.exp2(-jnp.abs(z))) * LOG2E
        sp = _top_rows(lambda u: jnp.where(strict, u, 0.0), sp, fresh_top)
        hi = sp.astype(BF16)
        lo = (sp - hi.astype(F32)).astype(BF16)
        cs = _dot(hi, incl) + _dot(lo, incl)
        a = jnp.exp2(z - cs + _wide(carry_in))
        a = _top_rows(lambda u: jnp.where(strict, u, 0.0), a, fresh_top)
        pv = _dot(a.astype(BF16), vb)
        return carry_in - _rep(cs[:, 0:1]), pv

    def set_alive(hh, i, carry):
        alive_ref[hh, i] = (jnp.max(carry) > ZERO_WEIGHT_LOG2).astype(jnp.int32)

    zeros = jnp.zeros((t, LANES), F32)
    last = (nq - 1) * t
    for hh in range(2):
        carry, pv = tile(hh, last, t, nq - 1, zeros, True)
        carry_ref[hh, pl.ds(last, t), :] = carry
        acc_ref[hh, pl.ds(last, t), :] = pv

    def key_block(jj, _):
        j = nq - 2 - jj
        top = pl.multiple_of(j * t, t)
        bot = pl.multiple_of(j * t + t, t)
        for hh in range(2):
            carry_in = jnp.concatenate([zeros, carry_ref[hh, pl.ds(bot, t), :]], axis=0)
            carry, pv = tile(hh, top, 2 * t, j, carry_in, True)
            carry_ref[hh, pl.ds(top, 2 * t), :] = carry
            acc_ref[hh, pl.ds(top, t), :] = pv[:t]
            acc_ref[hh, pl.ds(bot, t), :] += pv[t:]
            set_alive(hh, j + 1, carry[t:])

        def later_query_block(i, _):
            rows = pl.ds(pl.multiple_of(i * t, t), t)
            for hh in range(2):
                @pl.when(alive_ref[hh, i] != 0)
                def _():
                    carry, pv = tile(hh, pl.multiple_of(i * t, t), t, j,
                                     carry_ref[hh, rows, :], False)
                    carry_ref[hh, rows, :] = carry
                    acc_ref[hh, rows, :] += pv
                    set_alive(hh, i, carry)
            return 0

        lax.fori_loop(j + 2, nq, later_query_block, 0)
        return 0

    lax.fori_loop(0, nq - 1, key_block, 0)

    def finish(i, _):
        rows = pl.ds(pl.multiple_of(i * t, t), t)
        o = jnp.where(lane < HEAD_DIM, acc_ref[0, rows, :], acc_ref[1, rows, :])
        y_ref[0, rows, :] = _finish(o, g_ref[0, rows, :], onw_ref[...], lane)
        return 0

    lax.fori_loop(0, nq, finish, 0)


def _fox_kernel(lend_ref, q_ref, k_ref, v_ref, lq_ref, lk_ref, g_ref, onw_ref, y_ref,
                lqb_ref, qn_ref, m_ref, l_ref, acc_ref, bound_ref):
    t = ATT_TILE
    nq = q_ref.shape[1] // t
    b = pl.program_id(0)
    hp = pl.program_id(1)
    lane = _lane_ids()
    head_ids = lax.broadcasted_iota(jnp.int32, (1, FOX_HEADS), 1)
    row, col = _block_masks()
    causal = col <= row

    def head_sq_norm(x, hh):
        xf = x.astype(F32)
        return jnp.sum(jnp.where(_head_lanes(lane, hh), xf * xf, 0.0), axis=-1, keepdims=True)

    kmax = []
    for hh in range(2):
        h = 2 * hp + hh

        def prep(i, kmax2, hh=hh, h=h):
            rows = pl.ds(pl.multiple_of(i * t, t), t)
            lq = jnp.sum(jnp.where(head_ids == h, lq_ref[0, rows, :], 0.0), axis=-1, keepdims=True)
            lqb_ref[hh, rows, :] = _rep(lq)
            qn_ref[hh, rows, :] = _rep(jnp.sqrt(head_sq_norm(q_ref[0, rows, :], hh)))
            return jnp.maximum(kmax2, _rep(head_sq_norm(k_ref[0, rows, :], hh)))

        kmax2 = lax.fori_loop(0, nq, prep, jnp.zeros((t, LANES), F32))
        kmax.append(jnp.sqrt(jnp.max(kmax2, axis=0, keepdims=True)))

    def scores(hh, row0, nrows, j):
        h = 2 * hp + hh
        qh = _head_queries(q_ref[0, pl.ds(row0, nrows), :], lane, hh)
        start = pl.multiple_of(j * t, t)
        kb = k_ref[0, pl.ds(start, t), :]
        vb = v_ref[0, pl.ds(start, t), :]
        lk = lk_ref[0, h, pl.ds(j, 1), :]
        return _dot_nt(qh, kb) + _wide(lqb_ref[hh, pl.ds(row0, nrows), :]) - lk, vb

    def fresh(s):
        s = jnp.where(causal, s, MASKED)
        m = jnp.max(s, axis=-1, keepdims=True)
        p = jnp.exp2(s - m)
        return m, p, jnp.sum(p, axis=-1, keepdims=True)

    def update(s, rows, hh):
        m_old = m_ref[hh, rows, :]
        m_new = jnp.maximum(m_old, _rep(jnp.max(s, axis=-1, keepdims=True)))
        alpha = jnp.exp2(m_old - m_new)
        p = jnp.exp2(s - _wide(m_new))
        m_ref[hh, rows, :] = m_new
        l_ref[hh, rows, :] = alpha * l_ref[hh, rows, :] + _rep(jnp.sum(p, axis=-1, keepdims=True))
        return p, alpha

    def store_fresh(hh, i, rows, m, l, pv):
        m_ref[hh, rows, :] = _rep(m)
        l_ref[hh, rows, :] = _rep(l)
        acc_ref[hh, rows, :] = pv
        reach = qn_ref[hh, rows, :] * kmax[hh] + lqb_ref[hh, rows, :] - _rep(m)
        bound_ref[hh, i] = jnp.max(reach) + BOUND_SLACK_LOG2

    last = (nq - 1) * t
    for hh in range(2):
        s, vb = scores(hh, last, t, nq - 1)
        m, p, l = fresh(s)
        store_fresh(hh, nq - 1, pl.ds(last, t), m, l, _dot(p.astype(BF16), vb))

    def key_block(jj, _):
        j = nq - 2 - jj
        top = pl.multiple_of(j * t, t)
        bot = pl.multiple_of(j * t + t, t)
        for hh in range(2):
            s, vb = scores(hh, top, 2 * t, j)
            m, p_top, l = fresh(s[:t])
            p_bot, alpha = update(s[t:], pl.ds(bot, t), hh)
            pv = _dot(jnp.concatenate([p_top, p_bot], axis=0).astype(BF16), vb)
            store_fresh(hh, j, pl.ds(top, t), m, l, pv[:t])
            acc_ref[hh, pl.ds(bot, t), :] = alpha * acc_ref[hh, pl.ds(bot, t), :] + pv[t:]

        def later_query_block(i, _):
            rows = pl.ds(pl.multiple_of(i * t, t), t)
            for hh in range(2):
                h = 2 * hp + hh

                @pl.when(bound_ref[hh, i] - lend_ref[b * FOX_HEADS + h, j] >= ZERO_WEIGHT_LOG2)
                def _():
                    s, vb = scores(hh, pl.multiple_of(i * t, t), t, j)
                    p, alpha = update(s, rows, hh)
                    acc_ref[hh, rows, :] = alpha * acc_ref[hh, rows, :] + _dot(p.astype(BF16), vb)
            return 0

        lax.fori_loop(j + 2, nq, later_query_block, 0)
        return 0

    lax.fori_loop(0, nq - 1, key_block, 0)

    def finish(i, _):
        rows = pl.ds(pl.multiple_of(i * t, t), t)
        o = jnp.where(lane < HEAD_DIM, acc_ref[0, rows, :] / l_ref[0, rows, :],
                      acc_ref[1, rows, :] / l_ref[1, rows, :])
        y_ref[0, rows, :] = _finish(o, g_ref[0, rows, :], onw_ref[...], lane)
        return 0

    lax.fori_loop(0, nq, finish, 0)


def _mem_kernel(q_ref, k_ref, v_ref, g_ref, onw_ref, y_ref):
    q2 = q_ref[0]
    k2 = k_ref[0]
    v2 = v_ref[0]
    lane = _lane_ids()
    outs = []
    for hh in range(2):
        s = _dot_nt(_head_queries(q2, lane, hh), k2)
        p = jnp.exp2(s - jnp.max(s, axis=-1, keepdims=True))
        l = jnp.sum(p, axis=-1, keepdims=True)
        outs.append(_dot(p.astype(BF16), v2) / l)
    o = jnp.where(lane < HEAD_DIM, outs[0], outs[1])
    y_ref[0] = _finish(o, g_ref[0], onw_ref[...], lane)


def _seq_block(seq, first_block):
    return pl.BlockSpec((1, seq, LANES), lambda b, hp, *_: (b, 0, first_block + hp))


def _onw_block(first_block):
    return pl.BlockSpec((1, LANES), lambda b, hp, *_: (0, first_block + hp))


def _sb_attention(qkv, gate, onw, q_blk, k_blk, v_blk, g_blk):
    batch, seq, _ = qkv.shape
    nq = seq // ATT_TILE
    state = pltpu.VMEM((2, seq, LANES), F32)
    return pl.pallas_call(
        _sb_kernel,
        grid=(batch, SB_WIDTH // LANES),
        in_specs=[_seq_block(seq, q_blk), _seq_block(seq, k_blk), _seq_block(seq, v_blk),
                  _seq_block(seq, g_blk), _onw_block(g_blk)],
        out_specs=_seq_block(seq, 0),
        out_shape=jax.ShapeDtypeStruct((batch, seq, SB_WIDTH), BF16),
        scratch_shapes=[state, state, pltpu.SMEM((2, nq), jnp.int32)],
        compiler_params=_params("parallel", "parallel"),
        name="sb_attention",
    )(qkv, qkv, qkv, gate, onw)


def _fox_attention(lend, qkv, lq, lk, gate, onw, q_blk, k_blk, v_blk, g_blk):
    batch, seq, _ = qkv.shape
    nq = seq // ATT_TILE
    state = pltpu.VMEM((2, seq, LANES), F32)
    grid_spec = pltpu.PrefetchScalarGridSpec(
        num_scalar_prefetch=1,
        grid=(batch, FOX_WIDTH // LANES),
        in_specs=[_seq_block(seq, q_blk), _seq_block(seq, k_blk), _seq_block(seq, v_blk),
                  pl.BlockSpec((1, seq, FOX_HEADS), lambda b, hp, *_: (b, 0, 0)),
                  pl.BlockSpec((1, FOX_HEADS, nq, ATT_TILE), lambda b, hp, *_: (b, 0, 0, 0)),
                  _seq_block(seq, g_blk), _onw_block(g_blk)],
        out_specs=_seq_block(seq, 0),
        scratch_shapes=[state, state, state, state, state, pltpu.SMEM((2, nq), F32)],
    )
    return pl.pallas_call(
        _fox_kernel,
        grid_spec=grid_spec,
        out_shape=jax.ShapeDtypeStruct((batch, seq, FOX_WIDTH), BF16),
        compiler_params=_params("parallel", "parallel"),
        name="fox_attention",
    )(lend, qkv, qkv, qkv, lq, lk, gate, onw)


def _mem_attention(mq, mem_kv, gate, onw, g_blk):
    batch, seq, _ = mq.shape
    mem_len = mem_kv.shape[1]
    tile = lambda first: pl.BlockSpec((1, MEM_Q_TILE, LANES), lambda b, hp, i: (b, i, first + hp))
    kv = lambda first: pl.BlockSpec((1, mem_len, LANES), lambda b, hp, i: (b, 0, first + hp))
    return pl.pallas_call(
        _mem_kernel,
        grid=(batch, MEM_WIDTH // LANES, seq // MEM_Q_TILE),
        in_specs=[tile(0), kv(0), kv(MEM_WIDTH // LANES), tile(g_blk),
                  pl.BlockSpec((1, LANES), lambda b, hp, i: (0, g_blk + hp))],
        out_specs=tile(0),
        out_shape=jax.ShapeDtypeStruct((batch, seq, MEM_WIDTH), BF16),
        compiler_params=_params("parallel", "parallel", "arbitrary"),
        name="mem_attention",
    )(mq, mem_kv, mem_kv, gate, onw)


def kernel(x, mem, norm_w, w_in, b_forget, mem_norm_w, w_mem_kv, out_norm_w, w_out, final_norm_w):
    batch, seq, d = x.shape
    mem_len = mem.shape[1]
    depth = w_in.shape[0]
    m = batch * seq
    nblk = seq // ATT_TILE
    chunks = seq // CUM_CHUNK
    blocks = lambda width: width // LANES
    sb_q, sb_k, sb_v = 0, blocks(SB_WIDTH), 2 * blocks(SB_WIDTH)
    fx_q = 3 * blocks(SB_WIDTH)
    fx_k, fx_v = fx_q + blocks(FOX_WIDTH), fx_q + 2 * blocks(FOX_WIDTH)
    g_sb, g_fx, g_m = 0, blocks(SB_WIDTH), blocks(SB_WIDTH + FOX_WIDTH)

    ones = jnp.ones((SB_WIDTH,), F32)
    col_scale = jnp.concatenate([ones * Q_SCALE, ones, ones, ones * Q_SCALE, ones, ones])[None, :]
    off_f = QKV_WIDTH
    off_mq = off_f + FOX_HEADS
    off_g = off_mq + MEM_WIDTH

    x2d = x.reshape(m, d)
    mem2d = mem.reshape(batch * mem_len, d)
    for layer in range(depth):
        w = w_in[layer]
        wqkv = w[:, :QKV_WIDTH].astype(BF16)
        wf = jnp.pad(w[:, off_f:off_mq], ((0, 0), (0, LANES - FOX_HEADS))).astype(BF16)
        wmq = w[:, off_mq:off_g].astype(BF16)
        wg = w[:, off_g:].astype(BF16)
        qkv, mq, gate, f = _inproj(x2d, norm_w[layer][None, :], wqkv, wmq, wg, wf, col_scale)
        qkv = qkv.reshape(batch, seq, QKV_WIDTH)
        mq = mq.reshape(batch, seq, MEM_WIDTH)
        gate = gate.reshape(batch, seq, MIX_WIDTH)
        onw = out_norm_w[layer][None, :]

        f_rows = f[:, :FOX_HEADS].reshape(batch, seq, FOX_HEADS).transpose(0, 2, 1)
        f_rows = f_rows.reshape(batch * FOX_HEADS * chunks, CUM_CHUNK)
        bias_rows = jnp.repeat(b_forget[layer], chunks)[:, None]
        lfc = _fgate(f_rows, bias_rows, batch).reshape(batch, FOX_HEADS, seq)
        lk = lfc.reshape(batch, FOX_HEADS, nblk, ATT_TILE)
        lq = lfc.transpose(0, 2, 1)
        lend = lk[:, :, :, ATT_TILE - 1].reshape(batch * FOX_HEADS, nblk)

        mem_kv = _memkv(mem2d, mem_norm_w[layer][None, :], w_mem_kv[layer].astype(BF16))
        mem_kv = mem_kv.reshape(batch, mem_len, 2 * MEM_WIDTH)

        y_sb = _sb_attention(qkv, gate, onw, sb_q, sb_k, sb_v, g_sb)
        y_fx = _fox_attention(lend, qkv, lq, lk, gate, onw, fx_q, fx_k, fx_v, g_fx)
        y_m = _mem_attention(mq, mem_kv, gate, onw, g_m)

        x2d = _outproj(y_sb.reshape(m, SB_WIDTH), y_fx.reshape(m, FOX_WIDTH),
                       y_m.reshape(m, MEM_WIDTH), x2d, w_out[layer].astype(BF16),
                       final_norm_w[None, :], layer == depth - 1)
    return x2d.reshape(batch, seq, d)
```

```python
import functools

import jax
import jax.numpy as jnp
from jax import lax
from jax.experimental import pallas as pl
from jax.experimental.pallas import tpu as pltpu

HEAD_DIM = 64
SB_HEADS = 8
FOX_HEADS = 8
MEM_HEADS = 4
SB_WIDTH = SB_HEADS * HEAD_DIM
FOX_WIDTH = FOX_HEADS * HEAD_DIM
MEM_WIDTH = MEM_HEADS * HEAD_DIM
MIX_WIDTH = SB_WIDTH + FOX_WIDTH + MEM_WIDTH
QKV_WIDTH = 3 * SB_WIDTH + 3 * FOX_WIDTH
EPS = 1e-6
LOG2E = 1.4426950408889634
Q_SCALE = HEAD_DIM ** -0.5 * LOG2E
MASKED = -1e30
ZERO_WEIGHT_LOG2 = -150.0
BOUND_SLACK_LOG2 = 2.0
NORM_INFLATE = 1.0 + 2.0 ** -7

LANES = 128
ROW_TILE = 512
COL_CHUNK = 512
ATT_TILE = 256
MEM_Q_TILE = 512
CUM_CHUNK = 128
VMEM_LIMIT = 56 * 1024 * 1024

F32 = jnp.float32
BF16 = jnp.bfloat16
NT_DIMS = (((1,), (1,)), ((), ()))


def _params(*semantics):
    return pltpu.CompilerParams(dimension_semantics=semantics, vmem_limit_bytes=VMEM_LIMIT)


def _rmsnorm(x, w):
    return x * lax.rsqrt(jnp.mean(x * x, axis=-1, keepdims=True) + EPS) * w


def _dot(a, b):
    return jnp.dot(a, b, preferred_element_type=F32)


def _dot_nt(a, b):
    return lax.dot_general(a, b, NT_DIMS, preferred_element_type=F32)


def _inproj_kernel(x_ref, nw_ref, wqkv_ref, wmq_ref, wg_ref, wf_ref, cs_ref,
                   qkv_ref, mq_ref, gate_ref, f_ref):
    hb = _rmsnorm(x_ref[...], nw_ref[...]).astype(BF16)
    for c in range(QKV_WIDTH // COL_CHUNK):
        sl = slice(c * COL_CHUNK, (c + 1) * COL_CHUNK)
        qkv_ref[:, sl] = (_dot(hb, wqkv_ref[:, sl]) * cs_ref[:, sl]).astype(BF16)
    mq_ref[...] = (_dot(hb, wmq_ref[...]) * Q_SCALE).astype(BF16)
    for c in range(0, MIX_WIDTH, 256):
        sl = slice(c, c + 256)
        gate_ref[:, sl] = _dot(hb, wg_ref[:, sl])
    f_ref[...] = _dot(hb, wf_ref[...])


def _inproj(x2d, norm_w, wqkv, wmq, wg, wf, col_scale):
    m, d = x2d.shape
    full = lambda shape: pl.BlockSpec(shape, lambda i: (0, 0))
    rows = lambda n: pl.BlockSpec((ROW_TILE, n), lambda i: (i, 0))
    return pl.pallas_call(
        _inproj_kernel,
        grid=(m // ROW_TILE,),
        in_specs=[rows(d), full((1, d)), full(wqkv.shape), full(wmq.shape), full(wg.shape),
                  full(wf.shape), full((1, QKV_WIDTH))],
        out_specs=[rows(QKV_WIDTH), rows(MEM_WIDTH), rows(MIX_WIDTH), rows(LANES)],
        out_shape=[jax.ShapeDtypeStruct((m, QKV_WIDTH), BF16),
                   jax.ShapeDtypeStruct((m, MEM_WIDTH), BF16),
                   jax.ShapeDtypeStruct((m, MIX_WIDTH), F32),
                   jax.ShapeDtypeStruct((m, LANES), F32)],
        compiler_params=_params("parallel"),
        name="inproj",
    )(x2d, norm_w, wqkv, wmq, wg, wf, col_scale)


def _memkv_kernel(m_ref, nw_ref, w_ref, o_ref):
    hb = _rmsnorm(m_ref[...], nw_ref[...]).astype(BF16)
    o_ref[...] = _dot(hb, w_ref[...]).astype(BF16)


def _memkv(mem2d, norm_w, w):
    m, d = mem2d.shape
    n = w.shape[1]
    return pl.pallas_call(
        _memkv_kernel,
        grid=(m // ROW_TILE,),
        in_specs=[pl.BlockSpec((ROW_TILE, d), lambda i: (i, 0)),
                  pl.BlockSpec((1, d), lambda i: (0, 0)),
                  pl.BlockSpec((d, n), lambda i: (0, 0))],
        out_specs=pl.BlockSpec((ROW_TILE, n), lambda i: (i, 0)),
        out_shape=jax.ShapeDtypeStruct((m, n), BF16),
        compiler_params=_params("parallel"),
        name="memkv",
    )(mem2d, norm_w, w)


def _outproj_kernel(ysb_ref, yfx_ref, ym_ref, x_ref, w_ref, fnw_ref, o_ref, *, final):
    acc = x_ref[...]
    acc = acc + _dot(ysb_ref[...], w_ref[0:SB_WIDTH, :])
    acc = acc + _dot(yfx_ref[...], w_ref[SB_WIDTH:SB_WIDTH + FOX_WIDTH, :])
    acc = acc + _dot(ym_ref[...], w_ref[SB_WIDTH + FOX_WIDTH:MIX_WIDTH, :])
    if final:
        acc = _rmsnorm(acc, fnw_ref[...])
    o_ref[...] = acc


def _outproj(ysb, yfx, ym, x2d, w, final_norm_w, final):
    m, d = x2d.shape
    rows = lambda n: pl.BlockSpec((ROW_TILE, n), lambda i: (i, 0))
    return pl.pallas_call(
        functools.partial(_outproj_kernel, final=final),
        grid=(m // ROW_TILE,),
        in_specs=[rows(SB_WIDTH), rows(FOX_WIDTH), rows(MEM_WIDTH), rows(d),
                  pl.BlockSpec(w.shape, lambda i: (0, 0)),
                  pl.BlockSpec((1, d), lambda i: (0, 0))],
        out_specs=rows(d),
        out_shape=jax.ShapeDtypeStruct((m, d), F32),
        compiler_params=_params("parallel"),
        name="outproj",
    )(ysb, yfx, ym, x2d, w, final_norm_w)


def _split3(x):
    hi = x.astype(BF16)
    r = x - hi.astype(F32)
    mid = r.astype(BF16)
    lo = (r - mid.astype(F32)).astype(BF16)
    return hi, mid, lo


def _fgate_kernel(f_ref, b_ref, o_ref):
    t = f_ref[...] + b_ref[...]
    lf = (jnp.minimum(t, 0.0) - jnp.log(1.0 + jnp.exp(-jnp.abs(t)))) * LOG2E
    n = lf.shape[0]
    r = lax.broadcasted_iota(jnp.int32, (CUM_CHUNK, CUM_CHUNK), 0)
    c = lax.broadcasted_iota(jnp.int32, (CUM_CHUNK, CUM_CHUNK), 1)
    upper = (r <= c).astype(BF16)
    local = sum(_dot(p, upper) for p in _split3(lf))
    totals = jnp.broadcast_to(local[:, CUM_CHUNK - 1:CUM_CHUNK], (n, CUM_CHUNK))
    chunks = n // FOX_HEADS
    gr = lax.broadcasted_iota(jnp.int32, (n, n), 0)
    gc = lax.broadcasted_iota(jnp.int32, (n, n), 1)
    earlier = ((gr // chunks == gc // chunks) & (gc < gr)).astype(BF16)
    offs = sum(_dot(earlier, p) for p in _split3(totals))
    o_ref[...] = local + offs


def _fgate(f_rows, bias_rows, batch):
    n = f_rows.shape[0] // batch
    return pl.pallas_call(
        _fgate_kernel,
        grid=(batch,),
        in_specs=[pl.BlockSpec((n, CUM_CHUNK), lambda b: (b, 0)),
                  pl.BlockSpec((n, 1), lambda b: (0, 0))],
        out_specs=pl.BlockSpec((n, CUM_CHUNK), lambda b: (b, 0)),
        out_shape=jax.ShapeDtypeStruct(f_rows.shape, F32),
        compiler_params=_params("parallel"),
        name="fgate",
    )(f_rows, bias_rows)


def _lane_ids():
    return lax.broadcasted_iota(jnp.int32, (1, LANES), 1)


def _head_lanes(lane, hh):
    return (lane >= hh * HEAD_DIM) & (lane < (hh + 1) * HEAD_DIM)


def _head_queries(q2, lane, hh):
    return jnp.where(_head_lanes(lane, hh), q2, jnp.zeros_like(q2))


def _rep(col):
    return jnp.broadcast_to(col, (col.shape[0], LANES))


def _wide(x):
    return jnp.concatenate([x] * (ATT_TILE // LANES), axis=1)


def _finish(o, gate, onw, lane):
    first = lane < HEAD_DIM
    o2 = o * o
    s0 = jnp.sum(jnp.where(first, o2, 0.0), axis=-1, keepdims=True)
    s1 = jnp.sum(jnp.where(first, 0.0, o2), axis=-1, keepdims=True)
    ms = jnp.where(first, s0, s1) * (1.0 / HEAD_DIM)
    yn = o * lax.rsqrt(ms + EPS) * onw
    g = gate.astype(F32)
    return (yn * (g * jax.nn.sigmoid(g))).astype(BF16)


def _block_masks():
    t = ATT_TILE
    row = lax.broadcasted_iota(jnp.int32, (t, t), 0)
    col = lax.broadcasted_iota(jnp.int32, (t, t), 1)
    return row, col


def _top_rows(fn, x, fresh_top):
    if not fresh_top:
        return x
    if x.shape[0] == ATT_TILE:
        return fn(x)
    return jnp.concatenate([fn(x[:ATT_TILE]), x[ATT_TILE:]], axis=0)


def _sb_kernel(q_ref, k_ref, v_ref, g_ref, onw_ref, y_ref,
               carry0_ref, carry1_ref, acc0_ref, acc1_ref, alive_ref):
    t = ATT_TILE
    nq = q_ref.shape[1] // t
    assert nq % 2 == 0
    carry_refs = (carry0_ref, carry1_ref)
    acc_refs = (acc0_ref, acc1_ref)
    lane = _lane_ids()
    row, col = _block_masks()
    strict = col < row
    incl = (row >= col).astype(BF16)
    zeros = jnp.zeros((t, LANES), F32)
    heads = range(2)

    def rows_of(i):
        return pl.ds(pl.multiple_of(i * t, t), t)

    def mask_top(u):
        return jnp.where(strict, u, 0.0)

    def qk(hh, q, j):
        return _dot_nt(_head_queries(q, lane, hh), k_ref[0, rows_of(j), :])

    def softplus_parts(z, fresh_top):
        sp = jnp.maximum(z, 0.0) + jnp.log(1.0 + jnp.exp2(-jnp.abs(z))) * LOG2E
        sp = _top_rows(mask_top, sp, fresh_top)
        hi = sp.astype(BF16)
        return hi, (sp - hi.astype(F32)).astype(BF16)

    def suffix_sums(hi, lo):
        return _dot(hi, incl) + _dot(lo, incl)

    def weights(z, cs, carry_in, fresh_top):
        a = jnp.exp2(z - cs + _wide(carry_in))
        return _top_rows(mask_top, a, fresh_top).astype(BF16)

    def pv(a, j):
        return _dot(a, v_ref[0, rows_of(j), :])

    def set_alive(hh, i, carry):
        alive_ref[hh, i] = (jnp.max(carry) > ZERO_WEIGHT_LOG2).astype(jnp.int32)

    def later_query_blocks(j, first):
        def body(i, _):
            for hh in heads:
                @pl.when(alive_ref[hh, i] != 0)
                def _():
                    z = qk(hh, q_ref[0, rows_of(i), :], j)
                    cs = suffix_sums(*softplus_parts(z, False))
                    carry_in = carry_refs[hh][rows_of(i), :]
                    carry = carry_in - _rep(cs[:, 0:1])
                    acc_refs[hh][rows_of(i), :] += pv(weights(z, cs, carry_in, False), j)
                    carry_refs[hh][rows_of(i), :] = carry
                    set_alive(hh, i, carry)
            return 0

        lax.fori_loop(first, nq, body, 0)

    for hh in heads:
        carry_refs[hh][rows_of(nq - 1), :] = zeros
        acc_refs[hh][rows_of(nq - 1), :] = zeros

    def key_block_pair(jj, _):
        j = nq - 1 - 2 * jj
        ib = jnp.minimum(j + 1, nq - 1)
        q_j = q_ref[0, rows_of(j), :]
        qa = jnp.concatenate([q_j, q_ref[0, rows_of(ib), :]], axis=0)
        qb = jnp.concatenate([q_ref[0, rows_of(j - 1), :], q_j], axis=0)
        za = [qk(hh, qa, j) for hh in heads]
        zb = [qk(hh, qb, j - 1) for hh in heads]
        pa = [softplus_parts(z, True) for z in za]
        pb = [softplus_parts(z, True) for z in zb]
        ca = [suffix_sums(*p) for p in pa]
        cb = [suffix_sums(*p) for p in pb]
        wa, wb, carry_a, carry_b = [], [], [], []
        for hh in heads:
            in_a = jnp.concatenate([zeros, carry_refs[hh][rows_of(ib), :]], axis=0)
            out_a = in_a - _rep(ca[hh][:, 0:1])
            in_b = jnp.concatenate([zeros, out_a[:t]], axis=0)
            wa.append(weights(za[hh], ca[hh], in_a, True))
            wb.append(weights(zb[hh], cb[hh], in_b, True))
            carry_a.append(out_a)
            carry_b.append(in_b - _rep(cb[hh][:, 0:1]))
        pva = [pv(w, j) for w in wa]
        pvb = [pv(w, j - 1) for w in wb]
        for hh in heads:
            carry_refs[hh][rows_of(ib), :] = carry_a[hh][t:]
            acc_refs[hh][rows_of(ib), :] += pva[hh][t:]
            set_alive(hh, ib, carry_a[hh][t:])
            carry_refs[hh][rows_of(j), :] = carry_b[hh][t:]
            acc_refs[hh][rows_of(j), :] = pva[hh][:t] + pvb[hh][t:]
            set_alive(hh, j, carry_b[hh][t:])
            carry_refs[hh][rows_of(j - 1), :] = carry_b[hh][:t]
            acc_refs[hh][rows_of(j - 1), :] = pvb[hh][:t]
        later_query_blocks(j, j + 2)
        later_query_blocks(j - 1, j + 1)
        return 0

    lax.fori_loop(0, nq // 2, key_block_pair, 0)

    def finish(i, _):
        rows = rows_of(i)
        o = jnp.where(lane < HEAD_DIM, acc0_ref[rows, :], acc1_ref[rows, :])
        y_ref[0, rows, :] = _finish(o, g_ref[0, rows, :], onw_ref[...], lane)
        return 0

    lax.fori_loop(0, nq, finish, 0)


def _fox_kernel(lend_ref, q_ref, k_ref, v_ref, lk_ref, g_ref, onw_ref, y_ref,
                qn0_ref, qn1_ref, m0_ref, m1_ref, l0_ref, l1_ref, acc0_ref, acc1_ref, bound_ref):
    t = ATT_TILE
    nq = q_ref.shape[1] // t
    assert nq % 2 == 0
    qn_refs, m_refs = (qn0_ref, qn1_ref), (m0_ref, m1_ref)
    l_refs, acc_refs = (l0_ref, l1_ref), (acc0_ref, acc1_ref)
    hp = pl.program_id(1)
    heads = range(2)
    gate_row = [pl.program_id(0) * FOX_HEADS + 2 * hp + hh for hh in heads]
    lane = _lane_ids()
    row, col = _block_masks()
    causal = col <= row
    zeros = jnp.zeros((t, LANES), F32)

    def rows_of(i):
        return pl.ds(pl.multiple_of(i * t, t), t)

    li = lax.broadcasted_iota(jnp.int32, (LANES, 2 * LANES), 0)
    ci = lax.broadcasted_iota(jnp.int32, (LANES, 2 * LANES), 1)
    head_sum = (li // HEAD_DIM == ci // LANES).astype(BF16)

    def sq_norm_bounds(x):
        xf = x.astype(F32)
        return _dot((xf * xf).astype(BF16), head_sum) * NORM_INFLATE

    def prep(i, kmax2):
        qs = jnp.sqrt(sq_norm_bounds(q_ref[0, rows_of(i), :]))
        for hh in heads:
            qn_refs[hh][rows_of(i), :] = qs[:, hh * LANES:(hh + 1) * LANES]
        return jnp.maximum(kmax2, sq_norm_bounds(k_ref[0, rows_of(i), :]))

    kmax2 = lax.fori_loop(0, nq, prep, jnp.zeros((t, 2 * LANES), F32))
    kmax = jnp.sqrt(jnp.max(kmax2, axis=0, keepdims=True))
    kmax = [kmax[:, hh * LANES:(hh + 1) * LANES] for hh in heads]

    def qk(hh, q, j):
        return _dot_nt(_head_queries(q, lane, hh), k_ref[0, rows_of(j), :])

    def key_gates(hh, j):
        return lk_ref[0, 2 * hp + hh, pl.ds(j, 1), :]

    def halves(x):
        return x[:, :LANES] + x[:, LANES:]

    def fresh(s):
        s = jnp.where(causal, s, MASKED)
        m = jnp.max(s, axis=-1, keepdims=True)
        p = jnp.exp2(s - m)
        return _rep(m), p, halves(p)

    def update(s, m_old, l_old):
        m_new = jnp.maximum(m_old, _rep(jnp.max(s, axis=-1, keepdims=True)))
        alpha = jnp.exp2(m_old - m_new)
        p = jnp.exp2(s - _wide(m_new))
        return m_new, p, alpha * l_old + halves(p), alpha

    def pv(p, j):
        return _dot(p.astype(BF16), v_ref[0, rows_of(j), :])

    def set_bound(hh, i, m_diag):
        reach = qn_refs[hh][rows_of(i), :] * kmax[hh] - m_diag
        bound_ref[hh, i] = jnp.max(reach) + lend_ref[gate_row[hh], i] + BOUND_SLACK_LOG2

    def later_query_blocks(j, first):
        def body(i, _):
            for hh in heads:
                @pl.when(bound_ref[hh, i] - lend_ref[gate_row[hh], j] >= ZERO_WEIGHT_LOG2)
                def _():
                    s = qk(hh, q_ref[0, rows_of(i), :], j)
                    s = s + (lend_ref[gate_row[hh], i] - key_gates(hh, j))
                    m, p, l, alpha = update(s, m_refs[hh][rows_of(i), :], l_refs[hh][rows_of(i), :])
                    m_refs[hh][rows_of(i), :] = m
                    l_refs[hh][rows_of(i), :] = l
                    acc_refs[hh][rows_of(i), :] = alpha * acc_refs[hh][rows_of(i), :] + pv(p, j)
            return 0

        lax.fori_loop(first, nq, body, 0)

    for hh in heads:
        for ref in (m_refs[hh], l_refs[hh], acc_refs[hh]):
            ref[rows_of(nq - 1), :] = zeros

    def key_block_pair(jj, _):
        j = nq - 1 - 2 * jj
        ib = jnp.minimum(j + 1, nq - 1)
        q_j = q_ref[0, rows_of(j), :]
        qa = jnp.concatenate([q_j, q_ref[0, rows_of(ib), :]], axis=0)
        qb = jnp.concatenate([q_ref[0, rows_of(j - 1), :], q_j], axis=0)
        za = [qk(hh, qa, j) for hh in heads]
        zb = [qk(hh, qb, j - 1) for hh in heads]
        wa, wb, new = [], [], []
        for hh in heads:
            g = gate_row[hh]
            gates_a, gates_b = key_gates(hh, j), key_gates(hh, j - 1)
            m_a, p_a, l_a = fresh(za[hh][:t] + (lend_ref[g, j] - gates_a))
            m_ib, p_ib, l_ib, alpha_ib = update(za[hh][t:] + (lend_ref[g, ib] - gates_a),
                                                m_refs[hh][rows_of(ib), :], l_refs[hh][rows_of(ib), :])
            m_b, p_b, l_b = fresh(zb[hh][:t] + (lend_ref[g, j - 1] - gates_b))
            m_j, p_j, l_j, alpha_j = update(zb[hh][t:] + (lend_ref[g, j] - gates_b), m_a, l_a)
            wa.append(jnp.concatenate([p_a, p_ib], axis=0))
            wb.append(jnp.concatenate([p_b, p_j], axis=0))
            new.append((m_a, m_ib, l_ib, alpha_ib, m_b, l_b, m_j, l_j, alpha_j))
        pva = [pv(w, j) for w in wa]
        pvb = [pv(w, j - 1) for w in wb]
        for hh in heads:
            m_a, m_ib, l_ib, alpha_ib, m_b, l_b, m_j, l_j, alpha_j = new[hh]
            m_refs[hh][rows_of(ib), :] = m_ib
            l_refs[hh][rows_of(ib), :] = l_ib
            acc_refs[hh][rows_of(ib), :] = alpha_ib * acc_refs[hh][rows_of(ib), :] + pva[hh][t:]
            m_refs[hh][rows_of(j), :] = m_j
            l_refs[hh][rows_of(j), :] = l_j
            acc_refs[hh][rows_of(j), :] = alpha_j * pva[hh][:t] + pvb[hh][t:]
            set_bound(hh, j, m_a)
            m_refs[hh][rows_of(j - 1), :] = m_b
            l_refs[hh][rows_of(j - 1), :] = l_b
            acc_refs[hh][rows_of(j - 1), :] = pvb[hh][:t]
            set_bound(hh, j - 1, m_b)
        later_query_blocks(j, j + 2)
        later_query_blocks(j - 1, j + 1)
        return 0

    lax.fori_loop(0, nq // 2, key_block_pair, 0)

    def finish(i, _):
        rows = rows_of(i)
        sums = [jnp.sum(l_refs[hh][rows, :], axis=-1, keepdims=True) for hh in heads]
        o = jnp.where(lane < HEAD_DIM, acc0_ref[rows, :] / sums[0], acc1_ref[rows, :] / sums[1])
        y_ref[0, rows, :] = _finish(o, g_ref[0, rows, :], onw_ref[...], lane)
        return 0

    lax.fori_loop(0, nq, finish, 0)


def _mem_kernel(q_ref, k_ref, v_ref, g_ref, onw_ref, y_ref):
    q2 = q_ref[0]
    k2 = k_ref[0]
    v2 = v_ref[0]
    lane = _lane_ids()
    outs = []
    for hh in range(2):
        s = _dot_nt(_head_queries(q2, lane, hh), k2)
        p = jnp.exp2(s - jnp.max(s, axis=-1, keepdims=True))
        l = jnp.sum(p, axis=-1, keepdims=True)
        outs.append(_dot(p.astype(BF16), v2) / l)
    o = jnp.where(lane < HEAD_DIM, outs[0], outs[1])
    y_ref[0] = _finish(o, g_ref[0], onw_ref[...], lane)


def _seq_block(seq, first_block):
    return pl.BlockSpec((1, seq, LANES), lambda b, hp, *_: (b, 0, first_block + hp))


def _onw_block(first_block):
    return pl.BlockSpec((1, LANES), lambda b, hp, *_: (0, first_block + hp))


def _sb_attention(qkv, gate, onw, q_blk, k_blk, v_blk, g_blk):
    batch, seq, _ = qkv.shape
    nq = seq // ATT_TILE
    state = pltpu.VMEM((seq, LANES), F32)
    return pl.pallas_call(
        _sb_kernel,
        grid=(batch, SB_WIDTH // LANES),
        in_specs=[_seq_block(seq, q_blk), _seq_block(seq, k_blk), _seq_block(seq, v_blk),
                  _seq_block(seq, g_blk), _onw_block(g_blk)],
        out_specs=_seq_block(seq, 0),
        out_shape=jax.ShapeDtypeStruct((batch, seq, SB_WIDTH), BF16),
        scratch_shapes=[state, state, state, state, pltpu.SMEM((2, nq), jnp.int32)],
        compiler_params=_params("parallel", "parallel"),
        name="sb_attention",
    )(qkv, qkv, qkv, gate, onw)


def _fox_attention(lend, qkv, lk, gate, onw, q_blk, k_blk, v_blk, g_blk):
    batch, seq, _ = qkv.shape
    nq = seq // ATT_TILE
    state = pltpu.VMEM((seq, LANES), F32)
    grid_spec = pltpu.PrefetchScalarGridSpec(
        num_scalar_prefetch=1,
        grid=(batch, FOX_WIDTH // LANES),
        in_specs=[_seq_block(seq, q_blk), _seq_block(seq, k_blk), _seq_block(seq, v_blk),
                  pl.BlockSpec((1, FOX_HEADS, nq, ATT_TILE), lambda b, hp, *_: (b, 0, 0, 0)),
                  _seq_block(seq, g_blk), _onw_block(g_blk)],
        out_specs=_seq_block(seq, 0),
        scratch_shapes=[state] * 8 + [pltpu.SMEM((2, nq), F32)],
    )
    return pl.pallas_call(
        _fox_kernel,
        grid_spec=grid_spec,
        out_shape=jax.ShapeDtypeStruct((batch, seq, FOX_WIDTH), BF16),
        compiler_params=_params("parallel", "parallel"),
        name="fox_attention",
    )(lend, qkv, qkv, qkv, lk, gate, onw)


def _mem_attention(mq, mem_kv, gate, onw, g_blk):
    batch, seq, _ = mq.shape
    mem_len = mem_kv.shape[1]
    tile = lambda first: pl.BlockSpec((1, MEM_Q_TILE, LANES), lambda b, hp, i: (b, i, first + hp))
    kv = lambda first: pl.BlockSpec((1, mem_len, LANES), lambda b, hp, i: (b, 0, first + hp))
    return pl.pallas_call(
        _mem_kernel,
        grid=(batch, MEM_WIDTH // LANES, seq // MEM_Q_TILE),
        in_specs=[tile(0), kv(0), kv(MEM_WIDTH // LANES), tile(g_blk),
                  pl.BlockSpec((1, LANES), lambda b, hp, i: (0, g_blk + hp))],
        out_specs=tile(0),
        out_shape=jax.ShapeDtypeStruct((batch, seq, MEM_WIDTH), BF16),
        compiler_params=_params("parallel", "parallel", "arbitrary"),
        name="mem_attention",
    )(mq, mem_kv, mem_kv, gate, onw)


def kernel(x, mem, norm_w, w_in, b_forget, mem_norm_w, w_mem_kv, out_norm_w, w_out, final_norm_w):
    batch, seq, d = x.shape
    mem_len = mem.shape[1]
    depth = w_in.shape[0]
    m = batch * seq
    nblk = seq // ATT_TILE
    chunks = seq // CUM_CHUNK
    blocks = lambda width: width // LANES
    sb_q, sb_k, sb_v = 0, blocks(SB_WIDTH), 2 * blocks(SB_WIDTH)
    fx_q = 3 * blocks(SB_WIDTH)
    fx_k, fx_v = fx_q + blocks(FOX_WIDTH), fx_q + 2 * blocks(FOX_WIDTH)
    g_sb, g_fx, g_m = 0, blocks(SB_WIDTH), blocks(SB_WIDTH + FOX_WIDTH)

    ones = jnp.ones((SB_WIDTH,), F32)
    col_scale = jnp.concatenate([ones * Q_SCALE, ones, ones, ones * Q_SCALE, ones, ones])[None, :]
    off_f = QKV_WIDTH
    off_mq = off_f + FOX_HEADS
    off_g = off_mq + MEM_WIDTH

    x2d = x.reshape(m, d)
    mem2d = mem.reshape(batch * mem_len, d)
    for layer in range(depth):
        w = w_in[layer]
        wqkv = w[:, :QKV_WIDTH].astype(BF16)
        wf = jnp.pad(w[:, off_f:off_mq], ((0, 0), (0, LANES - FOX_HEADS))).astype(BF16)
        wmq = w[:, off_mq:off_g].astype(BF16)
        wg = w[:, off_g:].astype(BF16)
        qkv, mq, gate, f = _inproj(x2d, norm_w[layer][None, :], wqkv, wmq, wg, wf, col_scale)
        qkv = qkv.reshape(batch, seq, QKV_WIDTH)
        mq = mq.reshape(batch, seq, MEM_WIDTH)
        gate = gate.reshape(batch, seq, MIX_WIDTH)
        onw = out_norm_w[layer][None, :]

        f_rows = f[:, :FOX_HEADS].reshape(batch, seq, FOX_HEADS).transpose(0, 2, 1)
        f_rows = f_rows.reshape(batch * FOX_HEADS * chunks, CUM_CHUNK)
        bias_rows = jnp.repeat(b_forget[layer], chunks)[:, None]
        lfc = _fgate(f_rows, bias_rows, batch).reshape(batch, FOX_HEADS, seq)
        lk = lfc.reshape(batch, FOX_HEADS, nblk, ATT_TILE)
        lend = lk[:, :, :, ATT_TILE - 1].reshape(batch * FOX_HEADS, nblk)

        mem_kv = _memkv(mem2d, mem_norm_w[layer][None, :], w_mem_kv[layer].astype(BF16))
        mem_kv = mem_kv.reshape(batch, mem_len, 2 * MEM_WIDTH)

        y_sb = _sb_attention(qkv, gate, onw, sb_q, sb_k, sb_v, g_sb)
        y_fx = _fox_attention(lend, qkv, lk, gate, onw, fx_q, fx_k, fx_v, g_fx)
        y_m = _mem_attention(mq, mem_kv, gate, onw, g_m)

        x2d = _outproj(y_sb.reshape(m, SB_WIDTH), y_fx.reshape(m, FOX_WIDTH),
                       y_m.reshape(m, MEM_WIDTH), x2d, w_out[layer].astype(BF16),
                       final_norm_w[None, :], layer == depth - 1)
    return x2d.reshape(batch, seq, d)
```

```python
import functools

import jax
import jax.numpy as jnp
from jax import lax
from jax.experimental import pallas as pl
from jax.experimental.pallas import tpu as pltpu

HEAD_DIM = 64
SB_HEADS = 8
FOX_HEADS = 8
MEM_HEADS = 4
SB_WIDTH = SB_HEADS * HEAD_DIM
FOX_WIDTH = FOX_HEADS * HEAD_DIM
MEM_WIDTH = MEM_HEADS * HEAD_DIM
MIX_WIDTH = SB_WIDTH + FOX_WIDTH + MEM_WIDTH
QKV_WIDTH = 3 * SB_WIDTH + 3 * FOX_WIDTH
EPS = 1e-6
LOG2E = 1.4426950408889634
Q_SCALE = HEAD_DIM ** -0.5 * LOG2E
MASKED = -1e30
ZERO_WEIGHT_LOG2 = -150.0
BOUND_SLACK_LOG2 = 2.0
NORM_INFLATE = 1.0 + 2.0 ** -7

LANES = 128
ROW_TILE = 512
COL_CHUNK = 512
ATT_TILE = 256
MEM_Q_TILE = 512
CUM_CHUNK = 128
VMEM_LIMIT = 56 * 1024 * 1024

F32 = jnp.float32
BF16 = jnp.bfloat16
NT_DIMS = (((1,), (1,)), ((), ()))


def _params(*semantics):
    return pltpu.CompilerParams(dimension_semantics=semantics, vmem_limit_bytes=VMEM_LIMIT)


def _rmsnorm(x, w):
    return x * lax.rsqrt(jnp.mean(x * x, axis=-1, keepdims=True) + EPS) * w


def _dot(a, b):
    return jnp.dot(a, b, preferred_element_type=F32)


def _dot_nt(a, b):
    return lax.dot_general(a, b, NT_DIMS, preferred_element_type=F32)


def _inproj_kernel(x_ref, nw_ref, wqkv_ref, wmq_ref, wg_ref, wf_ref, cs_ref,
                   qkv_ref, mq_ref, gate_ref, f_ref):
    hb = _rmsnorm(x_ref[...], nw_ref[...]).astype(BF16)
    for c in range(QKV_WIDTH // COL_CHUNK):
        sl = slice(c * COL_CHUNK, (c + 1) * COL_CHUNK)
        qkv_ref[:, sl] = (_dot(hb, wqkv_ref[:, sl]) * cs_ref[:, sl]).astype(BF16)
    mq_ref[...] = (_dot(hb, wmq_ref[...]) * Q_SCALE).astype(BF16)
    for c in range(0, MIX_WIDTH, 256):
        sl = slice(c, c + 256)
        gate_ref[:, sl] = _dot(hb, wg_ref[:, sl])
    f_ref[...] = _dot(hb, wf_ref[...])


def _inproj(x2d, norm_w, wqkv, wmq, wg, wf, col_scale):
    m, d = x2d.shape
    full = lambda shape: pl.BlockSpec(shape, lambda i: (0, 0))
    rows = lambda n: pl.BlockSpec((ROW_TILE, n), lambda i: (i, 0))
    return pl.pallas_call(
        _inproj_kernel,
        grid=(m // ROW_TILE,),
        in_specs=[rows(d), full((1, d)), full(wqkv.shape), full(wmq.shape), full(wg.shape),
                  full(wf.shape), full((1, QKV_WIDTH))],
        out_specs=[rows(QKV_WIDTH), rows(MEM_WIDTH), rows(MIX_WIDTH), rows(LANES)],
        out_shape=[jax.ShapeDtypeStruct((m, QKV_WIDTH), BF16),
                   jax.ShapeDtypeStruct((m, MEM_WIDTH), BF16),
                   jax.ShapeDtypeStruct((m, MIX_WIDTH), F32),
                   jax.ShapeDtypeStruct((m, LANES), F32)],
        compiler_params=_params("parallel"),
        name="inproj",
    )(x2d, norm_w, wqkv, wmq, wg, wf, col_scale)


def _memkv_kernel(m_ref, nw_ref, w_ref, o_ref):
    hb = _rmsnorm(m_ref[...], nw_ref[...]).astype(BF16)
    o_ref[...] = _dot(hb, w_ref[...]).astype(BF16)


def _memkv(mem2d, norm_w, w):
    m, d = mem2d.shape
    n = w.shape[1]
    return pl.pallas_call(
        _memkv_kernel,
        grid=(m // ROW_TILE,),
        in_specs=[pl.BlockSpec((ROW_TILE, d), lambda i: (i, 0)),
                  pl.BlockSpec((1, d), lambda i: (0, 0)),
                  pl.BlockSpec((d, n), lambda i: (0, 0))],
        out_specs=pl.BlockSpec((ROW_TILE, n), lambda i: (i, 0)),
        out_shape=jax.ShapeDtypeStruct((m, n), BF16),
        compiler_params=_params("parallel"),
        name="memkv",
    )(mem2d, norm_w, w)


def _outproj_kernel(ysb_ref, yfx_ref, ym_ref, x_ref, w_ref, fnw_ref, o_ref, *, final):
    acc = x_ref[...]
    acc = acc + _dot(ysb_ref[...], w_ref[0:SB_WIDTH, :])
    acc = acc + _dot(yfx_ref[...], w_ref[SB_WIDTH:SB_WIDTH + FOX_WIDTH, :])
    acc = acc + _dot(ym_ref[...], w_ref[SB_WIDTH + FOX_WIDTH:MIX_WIDTH, :])
    if final:
        acc = _rmsnorm(acc, fnw_ref[...])
    o_ref[...] = acc


def _outproj(ysb, yfx, ym, x2d, w, final_norm_w, final):
    m, d = x2d.shape
    rows = lambda n: pl.BlockSpec((ROW_TILE, n), lambda i: (i, 0))
    return pl.pallas_call(
        functools.partial(_outproj_kernel, final=final),
        grid=(m // ROW_TILE,),
        in_specs=[rows(SB_WIDTH), rows(FOX_WIDTH), rows(MEM_WIDTH), rows(d),
                  pl.BlockSpec(w.shape, lambda i: (0, 0)),
                  pl.BlockSpec((1, d), lambda i: (0, 0))],
        out_specs=rows(d),
        out_shape=jax.ShapeDtypeStruct((m, d), F32),
        compiler_params=_params("parallel"),
        name="outproj",
    )(ysb, yfx, ym, x2d, w, final_norm_w)


def _split3(x):
    hi = x.astype(BF16)
    r = x - hi.astype(F32)
    mid = r.astype(BF16)
    lo = (r - mid.astype(F32)).astype(BF16)
    return hi, mid, lo


def _fgate_kernel(f_ref, b_ref, o_ref):
    t = f_ref[...] + b_ref[...]
    lf = (jnp.minimum(t, 0.0) - jnp.log(1.0 + jnp.exp(-jnp.abs(t)))) * LOG2E
    n = lf.shape[0]
    r = lax.broadcasted_iota(jnp.int32, (CUM_CHUNK, CUM_CHUNK), 0)
    c = lax.broadcasted_iota(jnp.int32, (CUM_CHUNK, CUM_CHUNK), 1)
    upper = (r <= c).astype(BF16)
    local = sum(_dot(p, upper) for p in _split3(lf))
    totals = jnp.broadcast_to(local[:, CUM_CHUNK - 1:CUM_CHUNK], (n, CUM_CHUNK))
    chunks = n // FOX_HEADS
    gr = lax.broadcasted_iota(jnp.int32, (n, n), 0)
    gc = lax.broadcasted_iota(jnp.int32, (n, n), 1)
    earlier = ((gr // chunks == gc // chunks) & (gc < gr)).astype(BF16)
    offs = sum(_dot(earlier, p) for p in _split3(totals))
    o_ref[...] = local + offs


def _fgate(f_rows, bias_rows, batch):
    n = f_rows.shape[0] // batch
    return pl.pallas_call(
        _fgate_kernel,
        grid=(batch,),
        in_specs=[pl.BlockSpec((n, CUM_CHUNK), lambda b: (b, 0)),
                  pl.BlockSpec((n, 1), lambda b: (0, 0))],
        out_specs=pl.BlockSpec((n, CUM_CHUNK), lambda b: (b, 0)),
        out_shape=jax.ShapeDtypeStruct(f_rows.shape, F32),
        compiler_params=_params("parallel"),
        name="fgate",
    )(f_rows, bias_rows)


def _lane_ids():
    return lax.broadcasted_iota(jnp.int32, (1, LANES), 1)


def _head_lanes(lane, hh):
    return (lane >= hh * HEAD_DIM) & (lane < (hh + 1) * HEAD_DIM)


def _head_queries(q2, lane, hh):
    return jnp.where(_head_lanes(lane, hh), q2, jnp.zeros_like(q2))


def _rep(col):
    return jnp.broadcast_to(col, (col.shape[0], LANES))


def _wide(x):
    return jnp.concatenate([x] * (ATT_TILE // LANES), axis=1)


def _split2(x):
    hi = x.astype(BF16)
    return hi, (x - hi.astype(F32)).astype(BF16)


def _group_mix(lanes_in, group_in, group_out):
    r = lax.broadcasted_iota(jnp.int32, (lanes_in, LANES), 0)
    c = lax.broadcasted_iota(jnp.int32, (lanes_in, LANES), 1)
    return (r // group_in == c // group_out).astype(BF16)


def _group_sum(x, mix):
    hi, lo = _split2(x)
    return _dot(hi, mix) + _dot(lo, mix)


def _head_mix():
    return _group_mix(LANES, HEAD_DIM, HEAD_DIM)


def _finish(o, gate, onw, head_mix):
    ms = _group_sum(o * o, head_mix) * (1.0 / HEAD_DIM)
    yn = o * lax.rsqrt(ms + EPS) * onw
    g = gate.astype(F32)
    return (yn * (g * jax.nn.sigmoid(g))).astype(BF16)


def _block_masks():
    t = ATT_TILE
    row = lax.broadcasted_iota(jnp.int32, (t, t), 0)
    col = lax.broadcasted_iota(jnp.int32, (t, t), 1)
    return row, col


def _top_rows(fn, x, fresh_top):
    if not fresh_top:
        return x
    if x.shape[0] == ATT_TILE:
        return fn(x)
    return jnp.concatenate([fn(x[:ATT_TILE]), x[ATT_TILE:]], axis=0)


def _sb_kernel(q_ref, k_ref, v_ref, g_ref, onw_ref, y_ref,
               carry0_ref, carry1_ref, acc0_ref, acc1_ref, alive_ref):
    t = ATT_TILE
    nq = q_ref.shape[1] // t
    assert nq % 2 == 0
    carry_refs = (carry0_ref, carry1_ref)
    acc_refs = (acc0_ref, acc1_ref)
    lane = _lane_ids()
    row, col = _block_masks()
    strict = col < row
    incl = (row >= col).astype(BF16)
    zeros = jnp.zeros((t, LANES), F32)
    heads = range(2)

    def rows_of(i):
        return pl.ds(pl.multiple_of(i * t, t), t)

    def mask_top(u):
        return jnp.where(strict, u, 0.0)

    def qk(hh, q, j):
        return _dot_nt(_head_queries(q, lane, hh), k_ref[0, rows_of(j), :])

    def softplus_parts(z, fresh_top):
        sp = jnp.maximum(z, 0.0) + jnp.log(1.0 + jnp.exp2(-jnp.abs(z))) * LOG2E
        sp = _top_rows(mask_top, sp, fresh_top)
        hi = sp.astype(BF16)
        return hi, (sp - hi.astype(F32)).astype(BF16)

    def suffix_sums(hi, lo):
        return _dot(hi, incl) + _dot(lo, incl)

    def weights(z, cs, carry_in, fresh_top):
        a = jnp.exp2(z - cs + _wide(carry_in))
        return _top_rows(mask_top, a, fresh_top).astype(BF16)

    def pv(a, j):
        return _dot(a, v_ref[0, rows_of(j), :])

    def set_alive(hh, i, carry):
        alive_ref[hh, i] = (jnp.max(carry) > ZERO_WEIGHT_LOG2).astype(jnp.int32)

    def later_query_blocks(j, first):
        def body(i, _):
            for hh in heads:
                @pl.when(alive_ref[hh, i] != 0)
                def _():
                    z = qk(hh, q_ref[0, rows_of(i), :], j)
                    cs = suffix_sums(*softplus_parts(z, False))
                    carry_in = carry_refs[hh][rows_of(i), :]
                    carry = carry_in - _rep(cs[:, 0:1])
                    acc_refs[hh][rows_of(i), :] += pv(weights(z, cs, carry_in, False), j)
                    carry_refs[hh][rows_of(i), :] = carry
                    set_alive(hh, i, carry)
            return 0

        lax.fori_loop(first, nq, body, 0)

    for hh in heads:
        carry_refs[hh][rows_of(nq - 1), :] = zeros
        acc_refs[hh][rows_of(nq - 1), :] = zeros

    def key_block_pair(jj, _):
        j = nq - 1 - 2 * jj
        ib = jnp.minimum(j + 1, nq - 1)
        q_j = q_ref[0, rows_of(j), :]
        qa = jnp.concatenate([q_j, q_ref[0, rows_of(ib), :]], axis=0)
        qb = jnp.concatenate([q_ref[0, rows_of(j - 1), :], q_j], axis=0)
        za = [qk(hh, qa, j) for hh in heads]
        zb = [qk(hh, qb, j - 1) for hh in heads]
        pa = [softplus_parts(z, True) for z in za]
        pb = [softplus_parts(z, True) for z in zb]
        ca = [suffix_sums(*p) for p in pa]
        cb = [suffix_sums(*p) for p in pb]
        wa, wb, carry_a, carry_b = [], [], [], []
        for hh in heads:
            in_a = jnp.concatenate([zeros, carry_refs[hh][rows_of(ib), :]], axis=0)
            out_a = in_a - _rep(ca[hh][:, 0:1])
            in_b = jnp.concatenate([zeros, out_a[:t]], axis=0)
            wa.append(weights(za[hh], ca[hh], in_a, True))
            wb.append(weights(zb[hh], cb[hh], in_b, True))
            carry_a.append(out_a)
            carry_b.append(in_b - _rep(cb[hh][:, 0:1]))
        pva = [pv(w, j) for w in wa]
        pvb = [pv(w, j - 1) for w in wb]
        for hh in heads:
            carry_refs[hh][rows_of(ib), :] = carry_a[hh][t:]
            acc_refs[hh][rows_of(ib), :] += pva[hh][t:]
            set_alive(hh, ib, carry_a[hh][t:])
            carry_refs[hh][rows_of(j), :] = carry_b[hh][t:]
            acc_refs[hh][rows_of(j), :] = pva[hh][:t] + pvb[hh][t:]
            set_alive(hh, j, carry_b[hh][t:])
            carry_refs[hh][rows_of(j - 1), :] = carry_b[hh][:t]
            acc_refs[hh][rows_of(j - 1), :] = pvb[hh][:t]
        later_query_blocks(j, j + 2)
        later_query_blocks(j - 1, j + 1)
        return 0

    lax.fori_loop(0, nq // 2, key_block_pair, 0)

    head_mix = _head_mix()

    def finish(i, _):
        rows = rows_of(i)
        o = jnp.where(lane < HEAD_DIM, acc0_ref[rows, :], acc1_ref[rows, :])
        y_ref[0, rows, :] = _finish(o, g_ref[0, rows, :], onw_ref[...], head_mix)
        return 0

    lax.fori_loop(0, nq, finish, 0, unroll=True)


def _fox_kernel(lend_ref, q_ref, k_ref, v_ref, lk_ref, g_ref, onw_ref, y_ref,
                qn0_ref, qn1_ref, m0_ref, m1_ref, l0_ref, l1_ref, acc0_ref, acc1_ref, bound_ref):
    t = ATT_TILE
    nq = q_ref.shape[1] // t
    assert nq % 2 == 0
    qn_refs, m_refs = (qn0_ref, qn1_ref), (m0_ref, m1_ref)
    l_refs, acc_refs = (l0_ref, l1_ref), (acc0_ref, acc1_ref)
    hp = pl.program_id(1)
    heads = range(2)
    gate_row = [pl.program_id(0) * FOX_HEADS + 2 * hp + hh for hh in heads]
    lane = _lane_ids()
    row, col = _block_masks()
    causal = col <= row
    zeros = jnp.zeros((t, LANES), F32)

    def rows_of(i):
        return pl.ds(pl.multiple_of(i * t, t), t)

    li = lax.broadcasted_iota(jnp.int32, (LANES, 2 * LANES), 0)
    ci = lax.broadcasted_iota(jnp.int32, (LANES, 2 * LANES), 1)
    head_sum = (li // HEAD_DIM == ci // LANES).astype(BF16)

    def prep(i, ksq_max):
        qf = q_ref[0, rows_of(i), :].astype(F32)
        qs = jnp.sqrt(_dot((qf * qf).astype(BF16), head_sum))
        for hh in heads:
            qn_refs[hh][rows_of(i), :] = qs[:, hh * LANES:(hh + 1) * LANES]
        kf = k_ref[0, rows_of(i), :].astype(F32)
        return jnp.maximum(ksq_max, kf * kf)

    ksq_max = lax.fori_loop(0, nq, prep, zeros, unroll=True)
    ksq_max = jnp.max(ksq_max, axis=0, keepdims=True)
    kmax = [jnp.sqrt(jnp.sum(jnp.where(_head_lanes(lane, hh), ksq_max, 0.0), axis=-1, keepdims=True))
            * NORM_INFLATE for hh in heads]

    def qk(hh, q, j):
        return _dot_nt(_head_queries(q, lane, hh), k_ref[0, rows_of(j), :])

    def key_gates(hh, j):
        return lk_ref[0, 2 * hp + hh, pl.ds(j, 1), :]

    def halves(x):
        return x[:, :LANES] + x[:, LANES:]

    def fresh(s):
        s = jnp.where(causal, s, MASKED)
        m = jnp.max(s, axis=-1, keepdims=True)
        p = jnp.exp2(s - m)
        return _rep(m), p, halves(p)

    def update(s, m_old, l_old):
        m_new = jnp.maximum(m_old, _rep(jnp.max(s, axis=-1, keepdims=True)))
        alpha = jnp.exp2(m_old - m_new)
        p = jnp.exp2(s - _wide(m_new))
        return m_new, p, alpha * l_old + halves(p), alpha

    def pv(p, j):
        return _dot(p.astype(BF16), v_ref[0, rows_of(j), :])

    def set_bound(hh, i, m_diag):
        reach = qn_refs[hh][rows_of(i), :] * kmax[hh] - m_diag
        bound_ref[hh, i] = jnp.max(reach) + lend_ref[gate_row[hh], i] + BOUND_SLACK_LOG2

    def later_query_blocks(j, first):
        def body(i, _):
            for hh in heads:
                @pl.when(bound_ref[hh, i] - lend_ref[gate_row[hh], j] >= ZERO_WEIGHT_LOG2)
                def _():
                    s = qk(hh, q_ref[0, rows_of(i), :], j)
                    s = s + (lend_ref[gate_row[hh], i] - key_gates(hh, j))
                    m, p, l, alpha = update(s, m_refs[hh][rows_of(i), :], l_refs[hh][rows_of(i), :])
                    m_refs[hh][rows_of(i), :] = m
                    l_refs[hh][rows_of(i), :] = l
                    acc_refs[hh][rows_of(i), :] = alpha * acc_refs[hh][rows_of(i), :] + pv(p, j)
            return 0

        lax.fori_loop(first, nq, body, 0)

    for hh in heads:
        for ref in (m_refs[hh], l_refs[hh], acc_refs[hh]):
            ref[rows_of(nq - 1), :] = zeros

    def key_block_pair(jj, _):
        j = nq - 1 - 2 * jj
        ib = jnp.minimum(j + 1, nq - 1)
        q_j = q_ref[0, rows_of(j), :]
        qa = jnp.concatenate([q_j, q_ref[0, rows_of(ib), :]], axis=0)
        qb = jnp.concatenate([q_ref[0, rows_of(j - 1), :], q_j], axis=0)
        za = [qk(hh, qa, j) for hh in heads]
        zb = [qk(hh, qb, j - 1) for hh in heads]
        wa, wb, new = [], [], []
        for hh in heads:
            g = gate_row[hh]
            gates_a, gates_b = key_gates(hh, j), key_gates(hh, j - 1)
            m_a, p_a, l_a = fresh(za[hh][:t] + (lend_ref[g, j] - gates_a))
            m_ib, p_ib, l_ib, alpha_ib = update(za[hh][t:] + (lend_ref[g, ib] - gates_a),
                                                m_refs[hh][rows_of(ib), :], l_refs[hh][rows_of(ib), :])
            m_b, p_b, l_b = fresh(zb[hh][:t] + (lend_ref[g, j - 1] - gates_b))
            m_j, p_j, l_j, alpha_j = update(zb[hh][t:] + (lend_ref[g, j] - gates_b), m_a, l_a)
            wa.append(jnp.concatenate([p_a, p_ib], axis=0))
            wb.append(jnp.concatenate([p_b, p_j], axis=0))
            new.append((m_a, m_ib, l_ib, alpha_ib, m_b, l_b, m_j, l_j, alpha_j))
        pva = [pv(w, j) for w in wa]
        pvb = [pv(w, j - 1) for w in wb]
        for hh in heads:
            m_a, m_ib, l_ib, alpha_ib, m_b, l_b, m_j, l_j, alpha_j = new[hh]
            m_refs[hh][rows_of(ib), :] = m_ib
            l_refs[hh][rows_of(ib), :] = l_ib
            acc_refs[hh][rows_of(ib), :] = alpha_ib * acc_refs[hh][rows_of(ib), :] + pva[hh][t:]
            m_refs[hh][rows_of(j), :] = m_j
            l_refs[hh][rows_of(j), :] = l_j
            acc_refs[hh][rows_of(j), :] = alpha_j * pva[hh][:t] + pvb[hh][t:]
            set_bound(hh, j, m_a)
            m_refs[hh][rows_of(j - 1), :] = m_b
            l_refs[hh][rows_of(j - 1), :] = l_b
            acc_refs[hh][rows_of(j - 1), :] = pvb[hh][:t]
            set_bound(hh, j - 1, m_b)
        later_query_blocks(j, j + 2)
        later_query_blocks(j - 1, j + 1)
        return 0

    lax.fori_loop(0, nq // 2, key_block_pair, 0)

    head_mix = _head_mix()
    pair_mix = _group_mix(2 * LANES, LANES, HEAD_DIM)

    def finish(i, _):
        rows = rows_of(i)
        partial = jnp.concatenate([l0_ref[rows, :], l1_ref[rows, :]], axis=1)
        acc = jnp.where(lane < HEAD_DIM, acc0_ref[rows, :], acc1_ref[rows, :])
        o = acc / _group_sum(partial, pair_mix)
        y_ref[0, rows, :] = _finish(o, g_ref[0, rows, :], onw_ref[...], head_mix)
        return 0

    lax.fori_loop(0, nq, finish, 0, unroll=True)


def _mem_kernel(q_ref, k_ref, v_ref, g_ref, onw_ref, y_ref):
    q2 = q_ref[0]
    k2 = k_ref[0]
    v2 = v_ref[0]
    lane = _lane_ids()
    outs = []
    for hh in range(2):
        s = _dot_nt(_head_queries(q2, lane, hh), k2)
        p = jnp.exp2(s - jnp.max(s, axis=-1, keepdims=True))
        l = jnp.sum(p, axis=-1, keepdims=True)
        outs.append(_dot(p.astype(BF16), v2) / l)
    o = jnp.where(lane < HEAD_DIM, outs[0], outs[1])
    y_ref[0] = _finish(o, g_ref[0], onw_ref[...], _head_mix())


def _seq_block(seq, first_block):
    return pl.BlockSpec((1, seq, LANES), lambda b, hp, *_: (b, 0, first_block + hp))


def _onw_block(first_block):
    return pl.BlockSpec((1, LANES), lambda b, hp, *_: (0, first_block + hp))


def _sb_attention(qkv, gate, onw, q_blk, k_blk, v_blk, g_blk):
    batch, seq, _ = qkv.shape
    nq = seq // ATT_TILE
    state = pltpu.VMEM((seq, LANES), F32)
    return pl.pallas_call(
        _sb_kernel,
        grid=(batch, SB_WIDTH // LANES),
        in_specs=[_seq_block(seq, q_blk), _seq_block(seq, k_blk), _seq_block(seq, v_blk),
                  _seq_block(seq, g_blk), _onw_block(g_blk)],
        out_specs=_seq_block(seq, 0),
        out_shape=jax.ShapeDtypeStruct((batch, seq, SB_WIDTH), BF16),
        scratch_shapes=[state, state, state, state, pltpu.SMEM((2, nq), jnp.int32)],
        compiler_params=_params("parallel", "parallel"),
        name="sb_attention",
    )(qkv, qkv, qkv, gate, onw)


def _fox_attention(lend, qkv, lk, gate, onw, q_blk, k_blk, v_blk, g_blk):
    batch, seq, _ = qkv.shape
    nq = seq // ATT_TILE
    state = pltpu.VMEM((seq, LANES), F32)
    grid_spec = pltpu.PrefetchScalarGridSpec(
        num_scalar_prefetch=1,
        grid=(batch, FOX_WIDTH // LANES),
        in_specs=[_seq_block(seq, q_blk), _seq_block(seq, k_blk), _seq_block(seq, v_blk),
                  pl.BlockSpec((1, FOX_HEADS, nq, ATT_TILE), lambda b, hp, *_: (b, 0, 0, 0)),
                  _seq_block(seq, g_blk), _onw_block(g_blk)],
        out_specs=_seq_block(seq, 0),
        scratch_shapes=[state] * 8 + [pltpu.SMEM((2, nq), F32)],
    )
    return pl.pallas_call(
        _fox_kernel,
        grid_spec=grid_spec,
        out_shape=jax.ShapeDtypeStruct((batch, seq, FOX_WIDTH), BF16),
        compiler_params=_params("parallel", "parallel"),
        name="fox_attention",
    )(lend, qkv, qkv, qkv, lk, gate, onw)


def _mem_attention(mq, mem_kv, gate, onw, g_blk):
    batch, seq, _ = mq.shape
    mem_len = mem_kv.shape[1]
    tile = lambda first: pl.BlockSpec((1, MEM_Q_TILE, LANES), lambda b, hp, i: (b, i, first + hp))
    kv = lambda first: pl.BlockSpec((1, mem_len, LANES), lambda b, hp, i: (b, 0, first + hp))
    return pl.pallas_call(
        _mem_kernel,
        grid=(batch, MEM_WIDTH // LANES, seq // MEM_Q_TILE),
        in_specs=[tile(0), kv(0), kv(MEM_WIDTH // LANES), tile(g_blk),
                  pl.BlockSpec((1, LANES), lambda b, hp, i: (0, g_blk + hp))],
        out_specs=tile(0),
        out_shape=jax.ShapeDtypeStruct((batch, seq, MEM_WIDTH), BF16),
        compiler_params=_params("parallel", "parallel", "arbitrary"),
        name="mem_attention",
    )(mq, mem_kv, mem_kv, gate, onw)


def kernel(x, mem, norm_w, w_in, b_forget, mem_norm_w, w_mem_kv, out_norm_w, w_out, final_norm_w):
    batch, seq, d = x.shape
    mem_len = mem.shape[1]
    depth = w_in.shape[0]
    m = batch * seq
    nblk = seq // ATT_TILE
    chunks = seq // CUM_CHUNK
    blocks = lambda width: width // LANES
    sb_q, sb_k, sb_v = 0, blocks(SB_WIDTH), 2 * blocks(SB_WIDTH)
    fx_q = 3 * blocks(SB_WIDTH)
    fx_k, fx_v = fx_q + blocks(FOX_WIDTH), fx_q + 2 * blocks(FOX_WIDTH)
    g_sb, g_fx, g_m = 0, blocks(SB_WIDTH), blocks(SB_WIDTH + FOX_WIDTH)

    ones = jnp.ones((SB_WIDTH,), F32)
    col_scale = jnp.concatenate([ones * Q_SCALE, ones, ones, ones * Q_SCALE, ones, ones])[None, :]
    off_f = QKV_WIDTH
    off_mq = off_f + FOX_HEADS
    off_g = off_mq + MEM_WIDTH

    x2d = x.reshape(m, d)
    mem2d = mem.reshape(batch * mem_len, d)
    for layer in range(depth):
        w = w_in[layer]
        wqkv = w[:, :QKV_WIDTH].astype(BF16)
        wf = jnp.pad(w[:, off_f:off_mq], ((0, 0), (0, LANES - FOX_HEADS))).astype(BF16)
        wmq = w[:, off_mq:off_g].astype(BF16)
        wg = w[:, off_g:].astype(BF16)
        qkv, mq, gate, f = _inproj(x2d, norm_w[layer][None, :], wqkv, wmq, wg, wf, col_scale)
        qkv = qkv.reshape(batch, seq, QKV_WIDTH)
        mq = mq.reshape(batch, seq, MEM_WIDTH)
        gate = gate.reshape(batch, seq, MIX_WIDTH)
        onw = out_norm_w[layer][None, :]

        f_rows = f[:, :FOX_HEADS].reshape(batch, seq, FOX_HEADS).transpose(0, 2, 1)
        f_rows = f_rows.reshape(batch * FOX_HEADS * chunks, CUM_CHUNK)
        bias_rows = jnp.repeat(b_forget[layer], chunks)[:, None]
        lfc = _fgate(f_rows, bias_rows, batch).reshape(batch, FOX_HEADS, seq)
        lk = lfc.reshape(batch, FOX_HEADS, nblk, ATT_TILE)
        lend = lk[:, :, :, ATT_TILE - 1].reshape(batch * FOX_HEADS, nblk)

        mem_kv = _memkv(mem2d, mem_norm_w[layer][None, :], w_mem_kv[layer].astype(BF16))
        mem_kv = mem_kv.reshape(batch, mem_len, 2 * MEM_WIDTH)

        y_sb = _sb_attention(qkv, gate, onw, sb_q, sb_k, sb_v, g_sb)
        y_fx = _fox_attention(lend, qkv, lk, gate, onw, fx_q, fx_k, fx_v, g_fx)
        y_m = _mem_attention(mq, mem_kv, gate, onw, g_m)

        x2d = _outproj(y_sb.reshape(m, SB_WIDTH), y_fx.reshape(m, FOX_WIDTH),
                       y_m.reshape(m, MEM_WIDTH), x2d, w_out[layer].astype(BF16),
                       final_norm_w[None, :], layer == depth - 1)
    return x2d.reshape(batch, seq, d)
```

```python
import functools

import jax
import jax.numpy as jnp
from jax import lax
from jax.experimental import pallas as pl
from jax.experimental.pallas import tpu as pltpu

HEAD_DIM = 64
SB_HEADS = 8
FOX_HEADS = 8
MEM_HEADS = 4
SB_WIDTH = SB_HEADS * HEAD_DIM
FOX_WIDTH = FOX_HEADS * HEAD_DIM
MEM_WIDTH = MEM_HEADS * HEAD_DIM
MIX_WIDTH = SB_WIDTH + FOX_WIDTH + MEM_WIDTH
QKV_WIDTH = 3 * SB_WIDTH + 3 * FOX_WIDTH
EPS = 1e-6
LOG2E = 1.4426950408889634
Q_SCALE = HEAD_DIM ** -0.5 * LOG2E
MASKED = -1e30
ZERO_WEIGHT_LOG2 = -150.0
BOUND_SLACK_LOG2 = 2.0
NORM_INFLATE = 1.0 + 2.0 ** -7

LANES = 128
ROW_TILE = 512
COL_CHUNK = 512
ATT_TILE = 256
MEM_Q_TILE = 512
CUM_CHUNK = 128
VMEM_LIMIT = 56 * 1024 * 1024

F32 = jnp.float32
BF16 = jnp.bfloat16
NT_DIMS = (((1,), (1,)), ((), ()))


def _params(*semantics):
    return pltpu.CompilerParams(dimension_semantics=semantics, vmem_limit_bytes=VMEM_LIMIT)


def _rmsnorm(x, w):
    return x * lax.rsqrt(jnp.mean(x * x, axis=-1, keepdims=True) + EPS) * w


def _dot(a, b):
    return jnp.dot(a, b, preferred_element_type=F32)


def _dot_nt(a, b):
    return lax.dot_general(a, b, NT_DIMS, preferred_element_type=F32)


def _inproj_kernel(x_ref, nw_ref, wqkv_ref, wmq_ref, wg_ref, wf_ref, cs_ref,
                   qkv_ref, mq_ref, gate_ref, f_ref):
    hb = _rmsnorm(x_ref[...], nw_ref[...]).astype(BF16)
    for c in range(QKV_WIDTH // COL_CHUNK):
        sl = slice(c * COL_CHUNK, (c + 1) * COL_CHUNK)
        qkv_ref[:, sl] = (_dot(hb, wqkv_ref[:, sl]) * cs_ref[:, sl]).astype(BF16)
    mq_ref[...] = (_dot(hb, wmq_ref[...]) * Q_SCALE).astype(BF16)
    for c in range(0, MIX_WIDTH, 256):
        sl = slice(c, c + 256)
        gate_ref[:, sl] = _dot(hb, wg_ref[:, sl])
    f_ref[...] = _dot(hb, wf_ref[...])


def _inproj(x2d, norm_w, wqkv, wmq, wg, wf, col_scale):
    m, d = x2d.shape
    full = lambda shape: pl.BlockSpec(shape, lambda i: (0, 0))
    rows = lambda n: pl.BlockSpec((ROW_TILE, n), lambda i: (i, 0))
    return pl.pallas_call(
        _inproj_kernel,
        grid=(m // ROW_TILE,),
        in_specs=[rows(d), full((1, d)), full(wqkv.shape), full(wmq.shape), full(wg.shape),
                  full(wf.shape), full((1, QKV_WIDTH))],
        out_specs=[rows(QKV_WIDTH), rows(MEM_WIDTH), rows(MIX_WIDTH), rows(LANES)],
        out_shape=[jax.ShapeDtypeStruct((m, QKV_WIDTH), BF16),
                   jax.ShapeDtypeStruct((m, MEM_WIDTH), BF16),
                   jax.ShapeDtypeStruct((m, MIX_WIDTH), F32),
                   jax.ShapeDtypeStruct((m, LANES), F32)],
        compiler_params=_params("parallel"),
        name="inproj",
    )(x2d, norm_w, wqkv, wmq, wg, wf, col_scale)


def _memkv_kernel(m_ref, nw_ref, w_ref, o_ref):
    hb = _rmsnorm(m_ref[...], nw_ref[...]).astype(BF16)
    o_ref[...] = _dot(hb, w_ref[...]).astype(BF16)


def _memkv(mem2d, norm_w, w):
    m, d = mem2d.shape
    n = w.shape[1]
    return pl.pallas_call(
        _memkv_kernel,
        grid=(m // ROW_TILE,),
        in_specs=[pl.BlockSpec((ROW_TILE, d), lambda i: (i, 0)),
                  pl.BlockSpec((1, d), lambda i: (0, 0)),
                  pl.BlockSpec((d, n), lambda i: (0, 0))],
        out_specs=pl.BlockSpec((ROW_TILE, n), lambda i: (i, 0)),
        out_shape=jax.ShapeDtypeStruct((m, n), BF16),
        compiler_params=_params("parallel"),
        name="memkv",
    )(mem2d, norm_w, w)


def _outproj_kernel(ysb_ref, yfx_ref, ym_ref, x_ref, w_ref, fnw_ref, o_ref, *, final):
    acc = x_ref[...]
    acc = acc + _dot(ysb_ref[...], w_ref[0:SB_WIDTH, :])
    acc = acc + _dot(yfx_ref[...], w_ref[SB_WIDTH:SB_WIDTH + FOX_WIDTH, :])
    acc = acc + _dot(ym_ref[...], w_ref[SB_WIDTH + FOX_WIDTH:MIX_WIDTH, :])
    if final:
        acc = _rmsnorm(acc, fnw_ref[...])
    o_ref[...] = acc


def _outproj(ysb, yfx, ym, x2d, w, final_norm_w, final):
    m, d = x2d.shape
    rows = lambda n: pl.BlockSpec((ROW_TILE, n), lambda i: (i, 0))
    return pl.pallas_call(
        functools.partial(_outproj_kernel, final=final),
        grid=(m // ROW_TILE,),
        in_specs=[rows(SB_WIDTH), rows(FOX_WIDTH), rows(MEM_WIDTH), rows(d),
                  pl.BlockSpec(w.shape, lambda i: (0, 0)),
                  pl.BlockSpec((1, d), lambda i: (0, 0))],
        out_specs=rows(d),
        out_shape=jax.ShapeDtypeStruct((m, d), F32),
        compiler_params=_params("parallel"),
        name="outproj",
    )(ysb, yfx, ym, x2d, w, final_norm_w)


def _split3(x):
    hi = x.astype(BF16)
    r = x - hi.astype(F32)
    mid = r.astype(BF16)
    lo = (r - mid.astype(F32)).astype(BF16)
    return hi, mid, lo


def _fgate_kernel(f_ref, b_ref, o_ref):
    t = f_ref[...] + b_ref[...]
    lf = (jnp.minimum(t, 0.0) - jnp.log(1.0 + jnp.exp(-jnp.abs(t)))) * LOG2E
    n = lf.shape[0]
    r = lax.broadcasted_iota(jnp.int32, (CUM_CHUNK, CUM_CHUNK), 0)
    c = lax.broadcasted_iota(jnp.int32, (CUM_CHUNK, CUM_CHUNK), 1)
    upper = (r <= c).astype(BF16)
    local = sum(_dot(p, upper) for p in _split3(lf))
    totals = jnp.broadcast_to(local[:, CUM_CHUNK - 1:CUM_CHUNK], (n, CUM_CHUNK))
    chunks = n // FOX_HEADS
    gr = lax.broadcasted_iota(jnp.int32, (n, n), 0)
    gc = lax.broadcasted_iota(jnp.int32, (n, n), 1)
    earlier = ((gr // chunks == gc // chunks) & (gc < gr)).astype(BF16)
    offs = sum(_dot(earlier, p) for p in _split3(totals))
    o_ref[...] = local + offs


def _fgate(f_rows, bias_rows, batch):
    n = f_rows.shape[0] // batch
    return pl.pallas_call(
        _fgate_kernel,
        grid=(batch,),
        in_specs=[pl.BlockSpec((n, CUM_CHUNK), lambda b: (b, 0)),
                  pl.BlockSpec((n, 1), lambda b: (0, 0))],
        out_specs=pl.BlockSpec((n, CUM_CHUNK), lambda b: (b, 0)),
        out_shape=jax.ShapeDtypeStruct(f_rows.shape, F32),
        compiler_params=_params("parallel"),
        name="fgate",
    )(f_rows, bias_rows)


def _lane_ids():
    return lax.broadcasted_iota(jnp.int32, (1, LANES), 1)


def _head_lanes(lane, hh):
    return (lane >= hh * HEAD_DIM) & (lane < (hh + 1) * HEAD_DIM)


def _head_queries(q2, lane, hh):
    return jnp.where(_head_lanes(lane, hh), q2, jnp.zeros_like(q2))


def _rep(col):
    return jnp.broadcast_to(col, (col.shape[0], LANES))


def _wide(x):
    return jnp.concatenate([x] * (ATT_TILE // LANES), axis=1)


def _split2(x):
    hi = x.astype(BF16)
    return hi, (x - hi.astype(F32)).astype(BF16)


def _group_mix(lanes_in, group_in, group_out):
    r = lax.broadcasted_iota(jnp.int32, (lanes_in, LANES), 0)
    c = lax.broadcasted_iota(jnp.int32, (lanes_in, LANES), 1)
    return (r // group_in == c // group_out).astype(BF16)


def _group_sum(x, mix):
    hi, lo = _split2(x)
    return _dot(hi, mix) + _dot(lo, mix)


def _head_mix():
    return _group_mix(LANES, HEAD_DIM, HEAD_DIM)


def _finish(o, gate, onw, head_mix):
    ms = _group_sum(o * o, head_mix) * (1.0 / HEAD_DIM)
    yn = o * lax.rsqrt(ms + EPS) * onw
    g = gate.astype(F32)
    return (yn * (g * jax.nn.sigmoid(g))).astype(BF16)


def _block_masks():
    t = ATT_TILE
    row = lax.broadcasted_iota(jnp.int32, (t, t), 0)
    col = lax.broadcasted_iota(jnp.int32, (t, t), 1)
    return row, col


def _top_rows(fn, x, fresh_top):
    if not fresh_top:
        return x
    if x.shape[0] == ATT_TILE:
        return fn(x)
    return jnp.concatenate([fn(x[:ATT_TILE]), x[ATT_TILE:]], axis=0)


def _sb_kernel(q_ref, k_ref, v_ref, g_ref, onw_ref, y_ref,
               carry0_ref, carry1_ref, acc0_ref, acc1_ref, alive_ref):
    t = ATT_TILE
    nq = q_ref.shape[1] // t
    assert nq % 2 == 0
    carry_refs = (carry0_ref, carry1_ref)
    acc_refs = (acc0_ref, acc1_ref)
    lane = _lane_ids()
    row, col = _block_masks()
    strict = col < row
    incl = (row >= col).astype(BF16)
    zeros = jnp.zeros((t, LANES), F32)
    heads = range(2)

    def rows_of(i):
        return pl.ds(pl.multiple_of(i * t, t), t)

    def mask_top(u):
        return jnp.where(strict, u, 0.0)

    def qk(hh, q, j):
        return _dot_nt(_head_queries(q, lane, hh), k_ref[0, rows_of(j), :])

    def softplus_parts(z, fresh_top):
        sp = jnp.maximum(z, 0.0) + jnp.log(1.0 + jnp.exp2(-jnp.abs(z))) * LOG2E
        return (_top_rows(mask_top, sp, fresh_top).astype(BF16),)

    def suffix_sums(sp):
        return _dot(sp, incl)

    def weights(z, cs, carry_in, fresh_top):
        a = jnp.exp2(z - cs + _wide(carry_in))
        return _top_rows(mask_top, a, fresh_top).astype(BF16)

    def pv(a, j):
        return _dot(a, v_ref[0, rows_of(j), :])

    def set_alive(hh, i, carry):
        alive_ref[hh, i] = (jnp.max(carry) > ZERO_WEIGHT_LOG2).astype(jnp.int32)

    def later_query_blocks(j, first):
        def body(i, _):
            for hh in heads:
                @pl.when(alive_ref[hh, i] != 0)
                def _():
                    z = qk(hh, q_ref[0, rows_of(i), :], j)
                    cs = suffix_sums(*softplus_parts(z, False))
                    carry_in = carry_refs[hh][rows_of(i), :]
                    carry = carry_in - _rep(cs[:, 0:1])
                    acc_refs[hh][rows_of(i), :] += pv(weights(z, cs, carry_in, False), j)
                    carry_refs[hh][rows_of(i), :] = carry
                    set_alive(hh, i, carry)
            return 0

        lax.fori_loop(first, nq, body, 0)

    for hh in heads:
        carry_refs[hh][rows_of(nq - 1), :] = zeros
        acc_refs[hh][rows_of(nq - 1), :] = zeros

    def key_block_pair(jj, _):
        j = nq - 1 - 2 * jj
        ib = jnp.minimum(j + 1, nq - 1)
        q_j = q_ref[0, rows_of(j), :]
        qa = jnp.concatenate([q_j, q_ref[0, rows_of(ib), :]], axis=0)
        qb = jnp.concatenate([q_ref[0, rows_of(j - 1), :], q_j], axis=0)
        za = [qk(hh, qa, j) for hh in heads]
        zb = [qk(hh, qb, j - 1) for hh in heads]
        pa = [softplus_parts(z, True) for z in za]
        pb = [softplus_parts(z, True) for z in zb]
        ca = [suffix_sums(*p) for p in pa]
        cb = [suffix_sums(*p) for p in pb]
        wa, wb, carry_a, carry_b = [], [], [], []
        for hh in heads:
            in_a = jnp.concatenate([zeros, carry_refs[hh][rows_of(ib), :]], axis=0)
            out_a = in_a - _rep(ca[hh][:, 0:1])
            in_b = jnp.concatenate([zeros, out_a[:t]], axis=0)
            wa.append(weights(za[hh], ca[hh], in_a, True))
            wb.append(weights(zb[hh], cb[hh], in_b, True))
            carry_a.append(out_a)
            carry_b.append(in_b - _rep(cb[hh][:, 0:1]))
        pva = [pv(w, j) for w in wa]
        pvb = [pv(w, j - 1) for w in wb]
        for hh in heads:
            carry_refs[hh][rows_of(ib), :] = carry_a[hh][t:]
            acc_refs[hh][rows_of(ib), :] += pva[hh][t:]
            set_alive(hh, ib, carry_a[hh][t:])
            carry_refs[hh][rows_of(j), :] = carry_b[hh][t:]
            acc_refs[hh][rows_of(j), :] = pva[hh][:t] + pvb[hh][t:]
            set_alive(hh, j, carry_b[hh][t:])
            carry_refs[hh][rows_of(j - 1), :] = carry_b[hh][:t]
            acc_refs[hh][rows_of(j - 1), :] = pvb[hh][:t]
        later_query_blocks(j, j + 2)
        later_query_blocks(j - 1, j + 1)
        return 0

    lax.fori_loop(0, nq // 2, key_block_pair, 0)

    head_mix = _head_mix()

    def finish(i, _):
        rows = rows_of(i)
        o = jnp.where(lane < HEAD_DIM, acc0_ref[rows, :], acc1_ref[rows, :])
        y_ref[0, rows, :] = _finish(o, g_ref[0, rows, :], onw_ref[...], head_mix)
        return 0

    lax.fori_loop(0, nq, finish, 0, unroll=True)


def _fox_kernel(lend_ref, q_ref, k_ref, v_ref, lk_ref, g_ref, onw_ref, y_ref,
                qn0_ref, qn1_ref, m0_ref, m1_ref, l0_ref, l1_ref, acc0_ref, acc1_ref, bound_ref):
    t = ATT_TILE
    nq = q_ref.shape[1] // t
    assert nq % 2 == 0
    qn_refs, m_refs = (qn0_ref, qn1_ref), (m0_ref, m1_ref)
    l_refs, acc_refs = (l0_ref, l1_ref), (acc0_ref, acc1_ref)
    hp = pl.program_id(1)
    heads = range(2)
    gate_row = [pl.program_id(0) * FOX_HEADS + 2 * hp + hh for hh in heads]
    lane = _lane_ids()
    row, col = _block_masks()
    causal = col <= row
    zeros = jnp.zeros((t, LANES), F32)

    def rows_of(i):
        return pl.ds(pl.multiple_of(i * t, t), t)

    li = lax.broadcasted_iota(jnp.int32, (LANES, 2 * LANES), 0)
    ci = lax.broadcasted_iota(jnp.int32, (LANES, 2 * LANES), 1)
    head_sum = (li // HEAD_DIM == ci // LANES).astype(BF16)

    def prep(i, ksq_max):
        qf = q_ref[0, rows_of(i), :].astype(F32)
        qs = jnp.sqrt(_dot((qf * qf).astype(BF16), head_sum))
        for hh in heads:
            qn_refs[hh][rows_of(i), :] = qs[:, hh * LANES:(hh + 1) * LANES]
        kf = k_ref[0, rows_of(i), :].astype(F32)
        return jnp.maximum(ksq_max, kf * kf)

    ksq_max = lax.fori_loop(0, nq, prep, zeros, unroll=True)
    ksq_max = jnp.max(ksq_max, axis=0, keepdims=True)
    kmax = [jnp.sqrt(jnp.sum(jnp.where(_head_lanes(lane, hh), ksq_max, 0.0), axis=-1, keepdims=True))
            * NORM_INFLATE for hh in heads]

    def qk(hh, q, j):
        return _dot_nt(_head_queries(q, lane, hh), k_ref[0, rows_of(j), :])

    def key_gates(hh, j):
        return lk_ref[0, 2 * hp + hh, pl.ds(j, 1), :]

    def halves(x):
        return x[:, :LANES] + x[:, LANES:]

    def fresh(s):
        s = jnp.where(causal, s, MASKED)
        m = jnp.max(s, axis=-1, keepdims=True)
        p = jnp.exp2(s - m)
        return _rep(m), p, halves(p)

    def update(s, m_old, l_old):
        m_new = jnp.maximum(m_old, _rep(jnp.max(s, axis=-1, keepdims=True)))
        alpha = jnp.exp2(m_old - m_new)
        p = jnp.exp2(s - _wide(m_new))
        return m_new, p, alpha * l_old + halves(p), alpha

    def pv(p, j):
        return _dot(p.astype(BF16), v_ref[0, rows_of(j), :])

    def set_bound(hh, i, m_diag):
        reach = qn_refs[hh][rows_of(i), :] * kmax[hh] - m_diag
        bound_ref[hh, i] = jnp.max(reach) + lend_ref[gate_row[hh], i] + BOUND_SLACK_LOG2

    def later_query_blocks(j, first):
        def body(i, _):
            for hh in heads:
                @pl.when(bound_ref[hh, i] - lend_ref[gate_row[hh], j] >= ZERO_WEIGHT_LOG2)
                def _():
                    s = qk(hh, q_ref[0, rows_of(i), :], j)
                    s = s + (lend_ref[gate_row[hh], i] - key_gates(hh, j))
                    m, p, l, alpha = update(s, m_refs[hh][rows_of(i), :], l_refs[hh][rows_of(i), :])
                    m_refs[hh][rows_of(i), :] = m
                    l_refs[hh][rows_of(i), :] = l
                    acc_refs[hh][rows_of(i), :] = alpha * acc_refs[hh][rows_of(i), :] + pv(p, j)
            return 0

        lax.fori_loop(first, nq, body, 0)

    for hh in heads:
        for ref in (m_refs[hh], l_refs[hh], acc_refs[hh]):
            ref[rows_of(nq - 1), :] = zeros

    def key_block_pair(jj, _):
        j = nq - 1 - 2 * jj
        ib = jnp.minimum(j + 1, nq - 1)
        q_j = q_ref[0, rows_of(j), :]
        qa = jnp.concatenate([q_j, q_ref[0, rows_of(ib), :]], axis=0)
        qb = jnp.concatenate([q_ref[0, rows_of(j - 1), :], q_j], axis=0)
        za = [qk(hh, qa, j) for hh in heads]
        zb = [qk(hh, qb, j - 1) for hh in heads]
        wa, wb, new = [], [], []
        for hh in heads:
            g = gate_row[hh]
            gates_a, gates_b = key_gates(hh, j), key_gates(hh, j - 1)
            m_a, p_a, l_a = fresh(za[hh][:t] + (lend_ref[g, j] - gates_a))
            m_ib, p_ib, l_ib, alpha_ib = update(za[hh][t:] + (lend_ref[g, ib] - gates_a),
                                                m_refs[hh][rows_of(ib), :], l_refs[hh][rows_of(ib), :])
            m_b, p_b, l_b = fresh(zb[hh][:t] + (lend_ref[g, j - 1] - gates_b))
            m_j, p_j, l_j, alpha_j = update(zb[hh][t:] + (lend_ref[g, j] - gates_b), m_a, l_a)
            wa.append(jnp.concatenate([p_a, p_ib], axis=0))
            wb.append(jnp.concatenate([p_b, p_j], axis=0))
            new.append((m_a, m_ib, l_ib, alpha_ib, m_b, l_b, m_j, l_j, alpha_j))
        pva = [pv(w, j) for w in wa]
        pvb = [pv(w, j - 1) for w in wb]
        for hh in heads:
            m_a, m_ib, l_ib, alpha_ib, m_b, l_b, m_j, l_j, alpha_j = new[hh]
            m_refs[hh][rows_of(ib), :] = m_ib
            l_refs[hh][rows_of(ib), :] = l_ib
            acc_refs[hh][rows_of(ib), :] = alpha_ib * acc_refs[hh][rows_of(ib), :] + pva[hh][t:]
            m_refs[hh][rows_of(j), :] = m_j
            l_refs[hh][rows_of(j), :] = l_j
            acc_refs[hh][rows_of(j), :] = alpha_j * pva[hh][:t] + pvb[hh][t:]
            set_bound(hh, j, m_a)
            m_refs[hh][rows_of(j - 1), :] = m_b
            l_refs[hh][rows_of(j - 1), :] = l_b
            acc_refs[hh][rows_of(j - 1), :] = pvb[hh][:t]
            set_bound(hh, j - 1, m_b)
        later_query_blocks(j, j + 2)
        later_query_blocks(j - 1, j + 1)
        return 0

    lax.fori_loop(0, nq // 2, key_block_pair, 0)

    head_mix = _head_mix()
    pair_mix = _group_mix(2 * LANES, LANES, HEAD_DIM)

    def finish(i, _):
        rows = rows_of(i)
        partial = jnp.concatenate([l0_ref[rows, :], l1_ref[rows, :]], axis=1)
        acc = jnp.where(lane < HEAD_DIM, acc0_ref[rows, :], acc1_ref[rows, :])
        o = acc / _group_sum(partial, pair_mix)
        y_ref[0, rows, :] = _finish(o, g_ref[0, rows, :], onw_ref[...], head_mix)
        return 0

    lax.fori_loop(0, nq, finish, 0, unroll=True)


def _mem_kernel(q_ref, k_ref, v_ref, g_ref, onw_ref, y_ref):
    q2 = q_ref[0]
    k2 = k_ref[0]
    v2 = v_ref[0]
    lane = _lane_ids()
    outs = []
    for hh in range(2):
        s = _dot_nt(_head_queries(q2, lane, hh), k2)
        p = jnp.exp2(s - jnp.max(s, axis=-1, keepdims=True))
        l = jnp.sum(p, axis=-1, keepdims=True)
        outs.append(_dot(p.astype(BF16), v2) / l)
    o = jnp.where(lane < HEAD_DIM, outs[0], outs[1])
    y_ref[0] = _finish(o, g_ref[0], onw_ref[...], _head_mix())


def _seq_block(seq, first_block):
    return pl.BlockSpec((1, seq, LANES), lambda b, hp, *_: (b, 0, first_block + hp))


def _onw_block(first_block):
    return pl.BlockSpec((1, LANES), lambda b, hp, *_: (0, first_block + hp))


def _sb_attention(qkv, gate, onw, q_blk, k_blk, v_blk, g_blk):
    batch, seq, _ = qkv.shape
    nq = seq // ATT_TILE
    state = pltpu.VMEM((seq, LANES), F32)
    return pl.pallas_call(
        _sb_kernel,
        grid=(batch, SB_WIDTH // LANES),
        in_specs=[_seq_block(seq, q_blk), _seq_block(seq, k_blk), _seq_block(seq, v_blk),
                  _seq_block(seq, g_blk), _onw_block(g_blk)],
        out_specs=_seq_block(seq, 0),
        out_shape=jax.ShapeDtypeStruct((batch, seq, SB_WIDTH), BF16),
        scratch_shapes=[state, state, state, state, pltpu.SMEM((2, nq), jnp.int32)],
        compiler_params=_params("parallel", "parallel"),
        name="sb_attention",
    )(qkv, qkv, qkv, gate, onw)


def _fox_attention(lend, qkv, lk, gate, onw, q_blk, k_blk, v_blk, g_blk):
    batch, seq, _ = qkv.shape
    nq = seq // ATT_TILE
    state = pltpu.VMEM((seq, LANES), F32)
    grid_spec = pltpu.PrefetchScalarGridSpec(
        num_scalar_prefetch=1,
        grid=(batch, FOX_WIDTH // LANES),
        in_specs=[_seq_block(seq, q_blk), _seq_block(seq, k_blk), _seq_block(seq, v_blk),
                  pl.BlockSpec((1, FOX_HEADS, nq, ATT_TILE), lambda b, hp, *_: (b, 0, 0, 0)),
                  _seq_block(seq, g_blk), _onw_block(g_blk)],
        out_specs=_seq_block(seq, 0),
        scratch_shapes=[state] * 8 + [pltpu.SMEM((2, nq), F32)],
    )
    return pl.pallas_call(
        _fox_kernel,
        grid_spec=grid_spec,
        out_shape=jax.ShapeDtypeStruct((batch, seq, FOX_WIDTH), BF16),
        compiler_params=_params("parallel", "parallel"),
        name="fox_attention",
    )(lend, qkv, qkv, qkv, lk, gate, onw)


def _mem_attention(mq, mem_kv, gate, onw, g_blk):
    batch, seq, _ = mq.shape
    mem_len = mem_kv.shape[1]
    tile = lambda first: pl.BlockSpec((1, MEM_Q_TILE, LANES), lambda b, hp, i: (b, i, first + hp))
    kv = lambda first: pl.BlockSpec((1, mem_len, LANES), lambda b, hp, i: (b, 0, first + hp))
    return pl.pallas_call(
        _mem_kernel,
        grid=(batch, MEM_WIDTH // LANES, seq // MEM_Q_TILE),
        in_specs=[tile(0), kv(0), kv(MEM_WIDTH // LANES), tile(g_blk),
                  pl.BlockSpec((1, LANES), lambda b, hp, i: (0, g_blk + hp))],
        out_specs=tile(0),
        out_shape=jax.ShapeDtypeStruct((batch, seq, MEM_WIDTH), BF16),
        compiler_params=_params("parallel", "parallel", "arbitrary"),
        name="mem_attention",
    )(mq, mem_kv, mem_kv, gate, onw)


def kernel(x, mem, norm_w, w_in, b_forget, mem_norm_w, w_mem_kv, out_norm_w, w_out, final_norm_w):
    batch, seq, d = x.shape
    mem_len = mem.shape[1]
    depth = w_in.shape[0]
    m = batch * seq
    nblk = seq // ATT_TILE
    chunks = seq // CUM_CHUNK
    blocks = lambda width: width // LANES
    sb_q, sb_k, sb_v = 0, blocks(SB_WIDTH), 2 * blocks(SB_WIDTH)
    fx_q = 3 * blocks(SB_WIDTH)
    fx_k, fx_v = fx_q + blocks(FOX_WIDTH), fx_q + 2 * blocks(FOX_WIDTH)
    g_sb, g_fx, g_m = 0, blocks(SB_WIDTH), blocks(SB_WIDTH + FOX_WIDTH)

    ones = jnp.ones((SB_WIDTH,), F32)
    col_scale = jnp.concatenate([ones * Q_SCALE, ones, ones, ones * Q_SCALE, ones, ones])[None, :]
    off_f = QKV_WIDTH
    off_mq = off_f + FOX_HEADS
    off_g = off_mq + MEM_WIDTH

    x2d = x.reshape(m, d)
    mem2d = mem.reshape(batch * mem_len, d)
    for layer in range(depth):
        w = w_in[layer]
        wqkv = w[:, :QKV_WIDTH].astype(BF16)
        wf = jnp.pad(w[:, off_f:off_mq], ((0, 0), (0, LANES - FOX_HEADS))).astype(BF16)
        wmq = w[:, off_mq:off_g].astype(BF16)
        wg = w[:, off_g:].astype(BF16)
        qkv, mq, gate, f = _inproj(x2d, norm_w[layer][None, :], wqkv, wmq, wg, wf, col_scale)
        qkv = qkv.reshape(batch, seq, QKV_WIDTH)
        mq = mq.reshape(batch, seq, MEM_WIDTH)
        gate = gate.reshape(batch, seq, MIX_WIDTH)
        onw = out_norm_w[layer][None, :]

        f_rows = f[:, :FOX_HEADS].reshape(batch, seq, FOX_HEADS).transpose(0, 2, 1)
        f_rows = f_rows.reshape(batch * FOX_HEADS * chunks, CUM_CHUNK)
        bias_rows = jnp.repeat(b_forget[layer], chunks)[:, None]
        lfc = _fgate(f_rows, bias_rows, batch).reshape(batch, FOX_HEADS, seq)
        lk = lfc.reshape(batch, FOX_HEADS, nblk, ATT_TILE)
        lend = lk[:, :, :, ATT_TILE - 1].reshape(batch * FOX_HEADS, nblk)

        mem_kv = _memkv(mem2d, mem_norm_w[layer][None, :], w_mem_kv[layer].astype(BF16))
        mem_kv = mem_kv.reshape(batch, mem_len, 2 * MEM_WIDTH)

        y_sb = _sb_attention(qkv, gate, onw, sb_q, sb_k, sb_v, g_sb)
        y_fx = _fox_attention(lend, qkv, lk, gate, onw, fx_q, fx_k, fx_v, g_fx)
        y_m = _mem_attention(mq, mem_kv, gate, onw, g_m)

        x2d = _outproj(y_sb.reshape(m, SB_WIDTH), y_fx.reshape(m, FOX_WIDTH),
                       y_m.reshape(m, MEM_WIDTH), x2d, w_out[layer].astype(BF16),
                       final_norm_w[None, :], layer == depth - 1)
    return x2d.reshape(batch, seq, d)
```

```python
import functools

import jax
import jax.numpy as jnp
from jax import lax
from jax.experimental import pallas as pl
from jax.experimental.pallas import tpu as pltpu

HEAD_DIM = 64
SB_HEADS = 8
FOX_HEADS = 8
MEM_HEADS = 4
SB_WIDTH = SB_HEADS * HEAD_DIM
FOX_WIDTH = FOX_HEADS * HEAD_DIM
MEM_WIDTH = MEM_HEADS * HEAD_DIM
MIX_WIDTH = SB_WIDTH + FOX_WIDTH + MEM_WIDTH
QKV_WIDTH = 3 * SB_WIDTH + 3 * FOX_WIDTH
EPS = 1e-6
LOG2E = 1.4426950408889634
Q_SCALE = HEAD_DIM ** -0.5 * LOG2E
MASKED = -1e30
ZERO_WEIGHT_LOG2 = -150.0
BOUND_SLACK_LOG2 = 2.0
NORM_INFLATE = 1.0 + 2.0 ** -7

LANES = 128
ROW_TILE = 512
COL_CHUNK = 512
ATT_TILE = 256
MEM_Q_TILE = 512
CUM_CHUNK = 128
VMEM_LIMIT = 56 * 1024 * 1024

F32 = jnp.float32
BF16 = jnp.bfloat16
NT_DIMS = (((1,), (1,)), ((), ()))


def _params(*semantics):
    return pltpu.CompilerParams(dimension_semantics=semantics, vmem_limit_bytes=VMEM_LIMIT)


def _rmsnorm(x, w):
    return x * lax.rsqrt(jnp.mean(x * x, axis=-1, keepdims=True) + EPS) * w


def _dot(a, b):
    return jnp.dot(a, b, preferred_element_type=F32)


def _dot_nt(a, b):
    return lax.dot_general(a, b, NT_DIMS, preferred_element_type=F32)


def _inproj_kernel(x_ref, nw_ref, wqkv_ref, wmq_ref, wg_ref, wf_ref, cs_ref,
                   qkv_ref, mq_ref, gate_ref, f_ref):
    hb = _rmsnorm(x_ref[...], nw_ref[...]).astype(BF16)
    for c in range(QKV_WIDTH // COL_CHUNK):
        sl = slice(c * COL_CHUNK, (c + 1) * COL_CHUNK)
        qkv_ref[:, sl] = (_dot(hb, wqkv_ref[:, sl]) * cs_ref[:, sl]).astype(BF16)
    mq_ref[...] = (_dot(hb, wmq_ref[...]) * Q_SCALE).astype(BF16)
    for c in range(0, MIX_WIDTH, 256):
        sl = slice(c, c + 256)
        gate_ref[:, sl] = _dot(hb, wg_ref[:, sl])
    f_ref[...] = _dot(hb, wf_ref[...])


def _inproj(x2d, norm_w, wqkv, wmq, wg, wf, col_scale):
    m, d = x2d.shape
    full = lambda shape: pl.BlockSpec(shape, lambda i: (0, 0))
    rows = lambda n: pl.BlockSpec((ROW_TILE, n), lambda i: (i, 0))
    return pl.pallas_call(
        _inproj_kernel,
        grid=(m // ROW_TILE,),
        in_specs=[rows(d), full((1, d)), full(wqkv.shape), full(wmq.shape), full(wg.shape),
                  full(wf.shape), full((1, QKV_WIDTH))],
        out_specs=[rows(QKV_WIDTH), rows(MEM_WIDTH), rows(MIX_WIDTH), rows(LANES)],
        out_shape=[jax.ShapeDtypeStruct((m, QKV_WIDTH), BF16),
                   jax.ShapeDtypeStruct((m, MEM_WIDTH), BF16),
                   jax.ShapeDtypeStruct((m, MIX_WIDTH), F32),
                   jax.ShapeDtypeStruct((m, LANES), F32)],
        compiler_params=_params("parallel"),
        name="inproj",
    )(x2d, norm_w, wqkv, wmq, wg, wf, col_scale)


def _memkv_kernel(m_ref, nw_ref, w_ref, o_ref):
    hb = _rmsnorm(m_ref[...], nw_ref[...]).astype(BF16)
    o_ref[...] = _dot(hb, w_ref[...]).astype(BF16)


def _memkv(mem2d, norm_w, w):
    m, d = mem2d.shape
    n = w.shape[1]
    return pl.pallas_call(
        _memkv_kernel,
        grid=(m // ROW_TILE,),
        in_specs=[pl.BlockSpec((ROW_TILE, d), lambda i: (i, 0)),
                  pl.BlockSpec((1, d), lambda i: (0, 0)),
                  pl.BlockSpec((d, n), lambda i: (0, 0))],
        out_specs=pl.BlockSpec((ROW_TILE, n), lambda i: (i, 0)),
        out_shape=jax.ShapeDtypeStruct((m, n), BF16),
        compiler_params=_params("parallel"),
        name="memkv",
    )(mem2d, norm_w, w)


def _outproj_kernel(ysb_ref, yfx_ref, ym_ref, x_ref, w_ref, fnw_ref, o_ref, *, final):
    acc = x_ref[...]
    acc = acc + _dot(ysb_ref[...], w_ref[0:SB_WIDTH, :])
    acc = acc + _dot(yfx_ref[...], w_ref[SB_WIDTH:SB_WIDTH + FOX_WIDTH, :])
    acc = acc + _dot(ym_ref[...], w_ref[SB_WIDTH + FOX_WIDTH:MIX_WIDTH, :])
    if final:
        acc = _rmsnorm(acc, fnw_ref[...])
    o_ref[...] = acc


def _outproj(ysb, yfx, ym, x2d, w, final_norm_w, final):
    m, d = x2d.shape
    rows = lambda n: pl.BlockSpec((ROW_TILE, n), lambda i: (i, 0))
    return pl.pallas_call(
        functools.partial(_outproj_kernel, final=final),
        grid=(m // ROW_TILE,),
        in_specs=[rows(SB_WIDTH), rows(FOX_WIDTH), rows(MEM_WIDTH), rows(d),
                  pl.BlockSpec(w.shape, lambda i: (0, 0)),
                  pl.BlockSpec((1, d), lambda i: (0, 0))],
        out_specs=rows(d),
        out_shape=jax.ShapeDtypeStruct((m, d), F32),
        compiler_params=_params("parallel"),
        name="outproj",
    )(ysb, yfx, ym, x2d, w, final_norm_w)


def _split3(x):
    hi = x.astype(BF16)
    r = x - hi.astype(F32)
    mid = r.astype(BF16)
    lo = (r - mid.astype(F32)).astype(BF16)
    return hi, mid, lo


def _fgate_kernel(f_ref, b_ref, o_ref):
    t = f_ref[...] + b_ref[...]
    lf = (jnp.minimum(t, 0.0) - jnp.log(1.0 + jnp.exp(-jnp.abs(t)))) * LOG2E
    n = lf.shape[0]
    r = lax.broadcasted_iota(jnp.int32, (CUM_CHUNK, CUM_CHUNK), 0)
    c = lax.broadcasted_iota(jnp.int32, (CUM_CHUNK, CUM_CHUNK), 1)
    upper = (r <= c).astype(BF16)
    local = sum(_dot(p, upper) for p in _split3(lf))
    totals = jnp.broadcast_to(local[:, CUM_CHUNK - 1:CUM_CHUNK], (n, CUM_CHUNK))
    chunks = n // FOX_HEADS
    gr = lax.broadcasted_iota(jnp.int32, (n, n), 0)
    gc = lax.broadcasted_iota(jnp.int32, (n, n), 1)
    earlier = ((gr // chunks == gc // chunks) & (gc < gr)).astype(BF16)
    offs = sum(_dot(earlier, p) for p in _split3(totals))
    o_ref[...] = local + offs


def _fgate(f_rows, bias_rows, batch):
    n = f_rows.shape[0] // batch
    return pl.pallas_call(
        _fgate_kernel,
        grid=(batch,),
        in_specs=[pl.BlockSpec((n, CUM_CHUNK), lambda b: (b, 0)),
                  pl.BlockSpec((n, 1), lambda b: (0, 0))],
        out_specs=pl.BlockSpec((n, CUM_CHUNK), lambda b: (b, 0)),
        out_shape=jax.ShapeDtypeStruct(f_rows.shape, F32),
        compiler_params=_params("parallel"),
        name="fgate",
    )(f_rows, bias_rows)


def _lane_ids():
    return lax.broadcasted_iota(jnp.int32, (1, LANES), 1)


def _head_lanes(lane, hh):
    return (lane >= hh * HEAD_DIM) & (lane < (hh + 1) * HEAD_DIM)


def _head_queries(q2, lane, hh):
    return jnp.where(_head_lanes(lane, hh), q2, jnp.zeros_like(q2))


def _rep(col):
    return jnp.broadcast_to(col, (col.shape[0], LANES))


def _wide(x):
    return jnp.concatenate([x] * (ATT_TILE // LANES), axis=1)


def _split2(x):
    hi = x.astype(BF16)
    return hi, (x - hi.astype(F32)).astype(BF16)


def _group_mix(lanes_in, group_in, group_out):
    r = lax.broadcasted_iota(jnp.int32, (lanes_in, LANES), 0)
    c = lax.broadcasted_iota(jnp.int32, (lanes_in, LANES), 1)
    return (r // group_in == c // group_out).astype(BF16)


def _group_sum(x, mix):
    hi, lo = _split2(x)
    return _dot(hi, mix) + _dot(lo, mix)


def _head_mix():
    return _group_mix(LANES, HEAD_DIM, HEAD_DIM)


def _finish(o, gate, onw, head_mix, denom=None):
    ms = _group_sum(o * o, head_mix) * (1.0 / HEAD_DIM)
    eps = EPS if denom is None else EPS * (denom * denom)
    yn = o * lax.rsqrt(ms + eps) * onw
    g = gate.astype(F32)
    return (yn * (g * jax.nn.sigmoid(g))).astype(BF16)


def _block_rows(i):
    start = i * ATT_TILE
    return pl.ds(start if isinstance(i, int) else pl.multiple_of(start, ATT_TILE), ATT_TILE)


def _neg_abs(x):
    bits = lax.bitcast_convert_type(x, jnp.uint32) | jnp.uint32(0x80000000)
    return lax.bitcast_convert_type(bits, F32)


def _block_masks():
    t = ATT_TILE
    row = lax.broadcasted_iota(jnp.int32, (t, t), 0)
    col = lax.broadcasted_iota(jnp.int32, (t, t), 1)
    return row, col


def _top_rows(fn, x, fresh_top):
    if not fresh_top:
        return x
    if x.shape[0] == ATT_TILE:
        return fn(x)
    return jnp.concatenate([fn(x[:ATT_TILE]), x[ATT_TILE:]], axis=0)


def _sb_kernel(q_ref, k_ref, v_ref, g_ref, onw_ref, y_ref,
               carry0_ref, carry1_ref, acc0_ref, acc1_ref, alive_ref):
    t = ATT_TILE
    nq = q_ref.shape[1] // t
    assert nq % 2 == 0
    carry_refs = (carry0_ref, carry1_ref)
    acc_refs = (acc0_ref, acc1_ref)
    lane = _lane_ids()
    row, col = _block_masks()
    strict = col < row
    incl = (row >= col).astype(BF16)
    zeros = jnp.zeros((t, LANES), F32)
    heads = range(2)

    rows_of = _block_rows

    def mask_top(u):
        return jnp.where(strict, u, 0.0)

    def qk(hh, q, j):
        return _dot_nt(_head_queries(q, lane, hh), k_ref[0, rows_of(j), :])

    def softplus_parts(z, fresh_top):
        sp = jnp.maximum(z, 0.0) + jnp.log(1.0 + jnp.exp2(-jnp.abs(z))) * LOG2E
        return (_top_rows(mask_top, sp, fresh_top).astype(BF16),)

    def suffix_sums(sp):
        return _dot(sp, incl)

    def weights(z, cs, fresh_top):
        return _top_rows(mask_top, jnp.exp2(z - cs), fresh_top).astype(BF16)

    def pv(a, j):
        return _dot(a, v_ref[0, rows_of(j), :])

    def set_alive(hh, i, carry):
        alive_ref[hh, i] = (jnp.max(carry) > ZERO_WEIGHT_LOG2).astype(jnp.int32)

    def later_key_blocks(i, _):
        def body(jj, _):
            j = i - 2 - jj
            for hh in heads:
                @pl.when(alive_ref[hh, i] != 0)
                def _():
                    z = qk(hh, q_ref[0, rows_of(i), :], j)
                    cs = suffix_sums(*softplus_parts(z, False))
                    carry_in = carry_refs[hh][rows_of(i), :]
                    carry = carry_in - _rep(cs[:, 0:1])
                    acc_refs[hh][rows_of(i), :] += pv(weights(z, cs, False), j) * jnp.exp2(carry_in)
                    carry_refs[hh][rows_of(i), :] = carry
                    set_alive(hh, i, carry)
            return 0

        lax.fori_loop(0, i - 1, body, 0)
        return 0

    for hh in heads:
        carry_refs[hh][rows_of(nq - 1), :] = zeros
        acc_refs[hh][rows_of(nq - 1), :] = zeros

    def key_block_pair(jj):
        j = nq - 1 - 2 * jj
        ib = min(j + 1, nq - 1)
        q_j = q_ref[0, rows_of(j), :]
        qa = jnp.concatenate([q_j, q_ref[0, rows_of(ib), :]], axis=0)
        qb = jnp.concatenate([q_ref[0, rows_of(j - 1), :], q_j], axis=0)
        za = [qk(hh, qa, j) for hh in heads]
        zb = [qk(hh, qb, j - 1) for hh in heads]
        pa = [softplus_parts(z, True) for z in za]
        pb = [softplus_parts(z, True) for z in zb]
        ca = [suffix_sums(*p) for p in pa]
        cb = [suffix_sums(*p) for p in pb]
        pva = [pv(weights(za[hh], ca[hh], True), j) for hh in heads]
        pvb = [pv(weights(zb[hh], cb[hh], True), j - 1) for hh in heads]
        for hh in heads:
            sums_a, sums_b = _rep(ca[hh][:, 0:1]), _rep(cb[hh][:, 0:1])
            carry_ib = carry_refs[hh][rows_of(ib), :]
            carry_j = -sums_a[:t]
            new_ib = carry_ib - sums_a[t:]
            new_j = carry_j - sums_b[t:]
            carry_refs[hh][rows_of(ib), :] = new_ib
            acc_refs[hh][rows_of(ib), :] += pva[hh][t:] * jnp.exp2(carry_ib)
            set_alive(hh, ib, new_ib)
            carry_refs[hh][rows_of(j), :] = new_j
            acc_refs[hh][rows_of(j), :] = pva[hh][:t] + pvb[hh][t:] * jnp.exp2(carry_j)
            set_alive(hh, j, new_j)
            carry_refs[hh][rows_of(j - 1), :] = -sums_b[:t]
            acc_refs[hh][rows_of(j - 1), :] = pvb[hh][:t]

    for jj in range(nq // 2):
        key_block_pair(jj)
    lax.fori_loop(2, nq, later_key_blocks, 0)

    head_mix = _head_mix()

    def finish(i, _):
        rows = rows_of(i)
        o = jnp.where(lane < HEAD_DIM, acc0_ref[rows, :], acc1_ref[rows, :])
        y_ref[0, rows, :] = _finish(o, g_ref[0, rows, :], onw_ref[...], head_mix)
        return 0

    lax.fori_loop(0, nq, finish, 0, unroll=True)


def _fox_kernel(lend_ref, q_ref, k_ref, v_ref, lk_ref, g_ref, onw_ref, y_ref,
                m0_ref, m1_ref, l0_ref, l1_ref, acc0_ref, acc1_ref, qsq_ref, bound_ref):
    t = ATT_TILE
    nq = q_ref.shape[1] // t
    assert nq % 2 == 0
    m_refs, l_refs, acc_refs = (m0_ref, m1_ref), (l0_ref, l1_ref), (acc0_ref, acc1_ref)
    hp = pl.program_id(1)
    heads = range(2)
    gate_row = [pl.program_id(0) * FOX_HEADS + 2 * hp + hh for hh in heads]
    lane = _lane_ids()
    row, col = _block_masks()
    causal = col <= row
    zeros = jnp.zeros((t, LANES), F32)
    rows_of = _block_rows

    ksq = jnp.zeros((t, LANES), BF16)
    for i in range(nq):
        qb = q_ref[0, rows_of(i), :]
        kb = k_ref[0, rows_of(i), :]
        qsq_ref[pl.ds(i, 1), :] = jnp.max(qb * qb, axis=0, keepdims=True).astype(F32)
        ksq = jnp.maximum(ksq, kb * kb)
    ksq = jnp.max(ksq, axis=0, keepdims=True).astype(F32)
    reach = []
    for hh in heads:
        mine = _head_lanes(lane, hh)
        k2 = jnp.sum(jnp.where(mine, ksq, 0.0), axis=-1, keepdims=True)
        q2 = jnp.sum(jnp.where(mine, qsq_ref[...], 0.0), axis=-1, keepdims=True)
        reach.append(jnp.sqrt(q2 * k2) * NORM_INFLATE)

    def qk(hh, q, j):
        return _dot_nt(_head_queries(q, lane, hh), k_ref[0, rows_of(j), :])

    def key_gates(hh, j):
        return lk_ref[0, 2 * hp + hh, pl.ds(j, 1), :]

    def halves(x):
        return x[:, :LANES] + x[:, LANES:]

    def fresh(s):
        s = jnp.where(causal, s, MASKED)
        m = jnp.max(s, axis=-1, keepdims=True)
        p = jnp.exp2(s - m)
        return _rep(m), p, halves(p)

    def update(s, m_old, l_old):
        m_new = jnp.maximum(m_old, _rep(jnp.max(s, axis=-1, keepdims=True)))
        alpha = jnp.exp2(m_old - m_new)
        p = jnp.exp2(s - _wide(m_new))
        return m_new, p, alpha * l_old + halves(p), alpha

    def pv(p, j):
        return _dot(p.astype(BF16), v_ref[0, rows_of(j), :])

    def set_bound(hh, i, m_diag):
        gap = reach[hh][i:i + 1, :] - jnp.min(m_diag, axis=0, keepdims=True)[:, 0:1]
        bound_ref[hh, i] = jnp.max(gap) + lend_ref[gate_row[hh], i] + BOUND_SLACK_LOG2

    def later_key_blocks(i, _):
        def body(jj, _):
            j = i - 2 - jj
            for hh in heads:
                @pl.when(bound_ref[hh, i] - lend_ref[gate_row[hh], j] >= ZERO_WEIGHT_LOG2)
                def _():
                    s = qk(hh, q_ref[0, rows_of(i), :], j)
                    s = s + (lend_ref[gate_row[hh], i] - key_gates(hh, j))
                    m, p, l, alpha = update(s, m_refs[hh][rows_of(i), :], l_refs[hh][rows_of(i), :])
                    m_refs[hh][rows_of(i), :] = m
                    l_refs[hh][rows_of(i), :] = l
                    acc_refs[hh][rows_of(i), :] = alpha * acc_refs[hh][rows_of(i), :] + pv(p, j)
            return 0

        lax.fori_loop(0, i - 1, body, 0)
        return 0

    for hh in heads:
        for ref in (m_refs[hh], l_refs[hh], acc_refs[hh]):
            ref[rows_of(nq - 1), :] = zeros

    def key_block_pair(jj):
        j = nq - 1 - 2 * jj
        ib = min(j + 1, nq - 1)
        q_j = q_ref[0, rows_of(j), :]
        qa = jnp.concatenate([q_j, q_ref[0, rows_of(ib), :]], axis=0)
        qb = jnp.concatenate([q_ref[0, rows_of(j - 1), :], q_j], axis=0)
        za = [qk(hh, qa, j) for hh in heads]
        zb = [qk(hh, qb, j - 1) for hh in heads]
        wa, wb, new = [], [], []
        for hh in heads:
            g = gate_row[hh]
            gates_a, gates_b = key_gates(hh, j), key_gates(hh, j - 1)
            m_a, p_a, l_a = fresh(za[hh][:t] + (lend_ref[g, j] - gates_a))
            m_ib, p_ib, l_ib, alpha_ib = update(za[hh][t:] + (lend_ref[g, ib] - gates_a),
                                                m_refs[hh][rows_of(ib), :], l_refs[hh][rows_of(ib), :])
            m_b, p_b, l_b = fresh(zb[hh][:t] + (lend_ref[g, j - 1] - gates_b))
            m_j, p_j, l_j, alpha_j = update(zb[hh][t:] + (lend_ref[g, j] - gates_b), m_a, l_a)
            wa.append(jnp.concatenate([p_a, p_ib], axis=0))
            wb.append(jnp.concatenate([p_b, p_j], axis=0))
            new.append((m_a, m_ib, l_ib, alpha_ib, m_b, l_b, m_j, l_j, alpha_j))
        pva = [pv(w, j) for w in wa]
        pvb = [pv(w, j - 1) for w in wb]
        for hh in heads:
            m_a, m_ib, l_ib, alpha_ib, m_b, l_b, m_j, l_j, alpha_j = new[hh]
            m_refs[hh][rows_of(ib), :] = m_ib
            l_refs[hh][rows_of(ib), :] = l_ib
            acc_refs[hh][rows_of(ib), :] = alpha_ib * acc_refs[hh][rows_of(ib), :] + pva[hh][t:]
            m_refs[hh][rows_of(j), :] = m_j
            l_refs[hh][rows_of(j), :] = l_j
            acc_refs[hh][rows_of(j), :] = alpha_j * pva[hh][:t] + pvb[hh][t:]
            set_bound(hh, j, m_a)
            m_refs[hh][rows_of(j - 1), :] = m_b
            l_refs[hh][rows_of(j - 1), :] = l_b
            acc_refs[hh][rows_of(j - 1), :] = pvb[hh][:t]
            set_bound(hh, j - 1, m_b)

    for jj in range(nq // 2):
        key_block_pair(jj)
    lax.fori_loop(2, nq, later_key_blocks, 0)

    head_mix = _head_mix()
    pair_mix = _group_mix(2 * LANES, LANES, HEAD_DIM)

    def finish(i, _):
        rows = rows_of(i)
        partial = jnp.concatenate([l0_ref[rows, :], l1_ref[rows, :]], axis=1)
        acc = jnp.where(lane < HEAD_DIM, acc0_ref[rows, :], acc1_ref[rows, :])
        denom = _dot(partial.astype(BF16), pair_mix)
        y_ref[0, rows, :] = _finish(acc, g_ref[0, rows, :], onw_ref[...], head_mix, denom)
        return 0

    lax.fori_loop(0, nq, finish, 0, unroll=True)


def _mem_kernel(q_ref, k_ref, v_ref, g_ref, onw_ref, y_ref):
    q2 = q_ref[0]
    k2 = k_ref[0]
    v2 = v_ref[0]
    lane = _lane_ids()
    outs, sums = [], []
    for hh in range(2):
        s = _dot_nt(_head_queries(q2, lane, hh), k2)
        p = jnp.exp2(s - jnp.max(s, axis=-1, keepdims=True))
        sums.append(jnp.sum(p, axis=-1, keepdims=True))
        outs.append(_dot(p.astype(BF16), v2))
    first = lane < HEAD_DIM
    o = jnp.where(first, outs[0], outs[1])
    denom = jnp.where(first, sums[0], sums[1])
    y_ref[0] = _finish(o, g_ref[0], onw_ref[...], _head_mix(), denom)


def _seq_block(seq, first_block):
    return pl.BlockSpec((1, seq, LANES), lambda b, hp, *_: (b, 0, first_block + hp))


def _onw_block(first_block):
    return pl.BlockSpec((1, LANES), lambda b, hp, *_: (0, first_block + hp))


def _sb_attention(qkv, gate, onw, q_blk, k_blk, v_blk, g_blk):
    batch, seq, _ = qkv.shape
    nq = seq // ATT_TILE
    state = pltpu.VMEM((seq, LANES), F32)
    return pl.pallas_call(
        _sb_kernel,
        grid=(batch, SB_WIDTH // LANES),
        in_specs=[_seq_block(seq, q_blk), _seq_block(seq, k_blk), _seq_block(seq, v_blk),
                  _seq_block(seq, g_blk), _onw_block(g_blk)],
        out_specs=_seq_block(seq, 0),
        out_shape=jax.ShapeDtypeStruct((batch, seq, SB_WIDTH), BF16),
        scratch_shapes=[state, state, state, state, pltpu.SMEM((2, nq), jnp.int32)],
        compiler_params=_params("parallel", "parallel"),
        name="sb_attention",
    )(qkv, qkv, qkv, gate, onw)


def _fox_attention(lend, qkv, lk, gate, onw, q_blk, k_blk, v_blk, g_blk):
    batch, seq, _ = qkv.shape
    nq = seq // ATT_TILE
    state = pltpu.VMEM((seq, LANES), F32)
    grid_spec = pltpu.PrefetchScalarGridSpec(
        num_scalar_prefetch=1,
        grid=(batch, FOX_WIDTH // LANES),
        in_specs=[_seq_block(seq, q_blk), _seq_block(seq, k_blk), _seq_block(seq, v_blk),
                  pl.BlockSpec((1, FOX_HEADS, nq, ATT_TILE), lambda b, hp, *_: (b, 0, 0, 0)),
                  _seq_block(seq, g_blk), _onw_block(g_blk)],
        out_specs=_seq_block(seq, 0),
        scratch_shapes=[state] * 6 + [pltpu.VMEM((nq, LANES), F32), pltpu.SMEM((2, nq), F32)],
    )
    return pl.pallas_call(
        _fox_kernel,
        grid_spec=grid_spec,
        out_shape=jax.ShapeDtypeStruct((batch, seq, FOX_WIDTH), BF16),
        compiler_params=_params("parallel", "parallel"),
        name="fox_attention",
    )(lend, qkv, qkv, qkv, lk, gate, onw)


def _mem_attention(mq, mem_kv, gate, onw, g_blk):
    batch, seq, _ = mq.shape
    mem_len = mem_kv.shape[1]
    tile = lambda first: pl.BlockSpec((1, MEM_Q_TILE, LANES), lambda b, hp, i: (b, i, first + hp))
    kv = lambda first: pl.BlockSpec((1, mem_len, LANES), lambda b, hp, i: (b, 0, first + hp))
    return pl.pallas_call(
        _mem_kernel,
        grid=(batch, MEM_WIDTH // LANES, seq // MEM_Q_TILE),
        in_specs=[tile(0), kv(0), kv(MEM_WIDTH // LANES), tile(g_blk),
                  pl.BlockSpec((1, LANES), lambda b, hp, i: (0, g_blk + hp))],
        out_specs=tile(0),
        out_shape=jax.ShapeDtypeStruct((batch, seq, MEM_WIDTH), BF16),
        compiler_params=_params("parallel", "parallel", "arbitrary"),
        name="mem_attention",
    )(mq, mem_kv, mem_kv, gate, onw)


def kernel(x, mem, norm_w, w_in, b_forget, mem_norm_w, w_mem_kv, out_norm_w, w_out, final_norm_w):
    batch, seq, d = x.shape
    mem_len = mem.shape[1]
    depth = w_in.shape[0]
    m = batch * seq
    nblk = seq // ATT_TILE
    chunks = seq // CUM_CHUNK
    blocks = lambda width: width // LANES
    sb_q, sb_k, sb_v = 0, blocks(SB_WIDTH), 2 * blocks(SB_WIDTH)
    fx_q = 3 * blocks(SB_WIDTH)
    fx_k, fx_v = fx_q + blocks(FOX_WIDTH), fx_q + 2 * blocks(FOX_WIDTH)
    g_sb, g_fx, g_m = 0, blocks(SB_WIDTH), blocks(SB_WIDTH + FOX_WIDTH)

    ones = jnp.ones((SB_WIDTH,), F32)
    col_scale = jnp.concatenate([ones * Q_SCALE, ones, ones, ones * Q_SCALE, ones, ones])[None, :]
    off_f = QKV_WIDTH
    off_mq = off_f + FOX_HEADS
    off_g = off_mq + MEM_WIDTH

    x2d = x.reshape(m, d)
    mem2d = mem.reshape(batch * mem_len, d)
    for layer in range(depth):
        w = w_in[layer]
        wqkv = w[:, :QKV_WIDTH].astype(BF16)
        wf = jnp.pad(w[:, off_f:off_mq], ((0, 0), (0, LANES - FOX_HEADS))).astype(BF16)
        wmq = w[:, off_mq:off_g].astype(BF16)
        wg = w[:, off_g:].astype(BF16)
        qkv, mq, gate, f = _inproj(x2d, norm_w[layer][None, :], wqkv, wmq, wg, wf, col_scale)
        qkv = qkv.reshape(batch, seq, QKV_WIDTH)
        mq = mq.reshape(batch, seq, MEM_WIDTH)
        gate = gate.reshape(batch, seq, MIX_WIDTH)
        onw = out_norm_w[layer][None, :]

        f_rows = f[:, :FOX_HEADS].reshape(batch, seq, FOX_HEADS).transpose(0, 2, 1)
        f_rows = f_rows.reshape(batch * FOX_HEADS * chunks, CUM_CHUNK)
        bias_rows = jnp.repeat(b_forget[layer], chunks)[:, None]
        lfc = _fgate(f_rows, bias_rows, batch).reshape(batch, FOX_HEADS, seq)
        lk = lfc.reshape(batch, FOX_HEADS, nblk, ATT_TILE)
        lend = lk[:, :, :, ATT_TILE - 1].reshape(batch * FOX_HEADS, nblk)

        mem_kv = _memkv(mem2d, mem_norm_w[layer][None, :], w_mem_kv[layer].astype(BF16))
        mem_kv = mem_kv.reshape(batch, mem_len, 2 * MEM_WIDTH)

        y_sb = _sb_attention(qkv, gate, onw, sb_q, sb_k, sb_v, g_sb)
        y_fx = _fox_attention(lend, qkv, lk, gate, onw, fx_q, fx_k, fx_v, g_fx)
        y_m = _mem_attention(mq, mem_kv, gate, onw, g_m)

        x2d = _outproj(y_sb.reshape(m, SB_WIDTH), y_fx.reshape(m, FOX_WIDTH),
                       y_m.reshape(m, MEM_WIDTH), x2d, w_out[layer].astype(BF16),
                       final_norm_w[None, :], layer == depth - 1)
    return x2d.reshape(batch, seq, d)
```

```python
import functools

import jax
import jax.numpy as jnp
from jax import lax
from jax.experimental import pallas as pl
from jax.experimental.pallas import tpu as pltpu

HEAD_DIM = 64
SB_HEADS = 8
FOX_HEADS = 8
MEM_HEADS = 4
SB_WIDTH = SB_HEADS * HEAD_DIM
FOX_WIDTH = FOX_HEADS * HEAD_DIM
MEM_WIDTH = MEM_HEADS * HEAD_DIM
MIX_WIDTH = SB_WIDTH + FOX_WIDTH + MEM_WIDTH
QKV_WIDTH = 3 * SB_WIDTH + 3 * FOX_WIDTH
EPS = 1e-6
LOG2E = 1.4426950408889634
Q_SCALE = HEAD_DIM ** -0.5 * LOG2E
MASKED = -1e30
ZERO_WEIGHT_LOG2 = -150.0
BOUND_SLACK_LOG2 = 2.0
NORM_INFLATE = 1.0 + 2.0 ** -7

LANES = 128
ROW_TILE = 512
COL_CHUNK = 512
ATT_TILE = 256
MEM_Q_TILE = 512
CUM_CHUNK = 128
VMEM_LIMIT = 56 * 1024 * 1024

F32 = jnp.float32
BF16 = jnp.bfloat16
NT_DIMS = (((1,), (1,)), ((), ()))


def _params(*semantics):
    return pltpu.CompilerParams(dimension_semantics=semantics, vmem_limit_bytes=VMEM_LIMIT)


def _rmsnorm(x, w):
    return x * lax.rsqrt(jnp.mean(x * x, axis=-1, keepdims=True) + EPS) * w


def _dot(a, b):
    return jnp.dot(a, b, preferred_element_type=F32)


def _dot_nt(a, b):
    return lax.dot_general(a, b, NT_DIMS, preferred_element_type=F32)


def _store_lane_blocks(ref, first_block, values):
    for c in range(values.shape[1] // LANES):
        ref[first_block + c] = values[:, c * LANES:(c + 1) * LANES]


def _lane_blocks(n_blocks):
    return pl.BlockSpec((n_blocks, ROW_TILE, LANES), lambda i: (0, i, 0))


def _inproj_kernel(x_ref, nw_ref, wqkv_ref, wmq_ref, wg_ref, wf_ref, cs_ref,
                   qkv_ref, mq_ref, gate_ref, f_ref):
    hb = _rmsnorm(x_ref[...], nw_ref[...]).astype(BF16)
    for c in range(0, QKV_WIDTH, COL_CHUNK):
        sl = slice(c, c + COL_CHUNK)
        _store_lane_blocks(qkv_ref, c // LANES,
                           (_dot(hb, wqkv_ref[:, sl]) * cs_ref[:, sl]).astype(BF16))
    _store_lane_blocks(mq_ref, 0, (_dot(hb, wmq_ref[...]) * Q_SCALE).astype(BF16))
    for c in range(0, MIX_WIDTH, 256):
        _store_lane_blocks(gate_ref, c // LANES, _dot(hb, wg_ref[:, c:c + 256]))
    f_ref[...] = _dot(hb, wf_ref[...])


def _inproj(x2d, norm_w, wqkv, wmq, wg, wf, col_scale):
    m, d = x2d.shape
    full = lambda shape: pl.BlockSpec(shape, lambda i: (0, 0))
    rows = lambda n: pl.BlockSpec((ROW_TILE, n), lambda i: (i, 0))
    blocks = lambda width: width // LANES
    return pl.pallas_call(
        _inproj_kernel,
        grid=(m // ROW_TILE,),
        in_specs=[rows(d), full((1, d)), full(wqkv.shape), full(wmq.shape), full(wg.shape),
                  full(wf.shape), full((1, QKV_WIDTH))],
        out_specs=[_lane_blocks(blocks(QKV_WIDTH)), _lane_blocks(blocks(MEM_WIDTH)),
                   _lane_blocks(blocks(MIX_WIDTH)), rows(LANES)],
        out_shape=[jax.ShapeDtypeStruct((blocks(QKV_WIDTH), m, LANES), BF16),
                   jax.ShapeDtypeStruct((blocks(MEM_WIDTH), m, LANES), BF16),
                   jax.ShapeDtypeStruct((blocks(MIX_WIDTH), m, LANES), F32),
                   jax.ShapeDtypeStruct((m, LANES), F32)],
        compiler_params=_params("parallel"),
        name="inproj",
    )(x2d, norm_w, wqkv, wmq, wg, wf, col_scale)


def _memkv_kernel(m_ref, nw_ref, w_ref, o_ref):
    hb = _rmsnorm(m_ref[...], nw_ref[...]).astype(BF16)
    _store_lane_blocks(o_ref, 0, _dot(hb, w_ref[...]).astype(BF16))


def _memkv(mem2d, norm_w, w):
    m, d = mem2d.shape
    n = w.shape[1]
    return pl.pallas_call(
        _memkv_kernel,
        grid=(m // ROW_TILE,),
        in_specs=[pl.BlockSpec((ROW_TILE, d), lambda i: (i, 0)),
                  pl.BlockSpec((1, d), lambda i: (0, 0)),
                  pl.BlockSpec((d, n), lambda i: (0, 0))],
        out_specs=_lane_blocks(n // LANES),
        out_shape=jax.ShapeDtypeStruct((n // LANES, m, LANES), BF16),
        compiler_params=_params("parallel"),
        name="memkv",
    )(mem2d, norm_w, w)


def _outproj_kernel(ysb_ref, yfx_ref, ym_ref, x_ref, w_ref, fnw_ref, o_ref, *, final):
    y = jnp.concatenate([ref[c] for ref in (ysb_ref, yfx_ref, ym_ref) for c in range(ref.shape[0])],
                        axis=1)
    acc = x_ref[...] + _dot(y, w_ref[...])
    if final:
        acc = _rmsnorm(acc, fnw_ref[...])
    o_ref[...] = acc


def _outproj(ysb, yfx, ym, x2d, w, final_norm_w, final):
    m, d = x2d.shape
    rows = lambda n: pl.BlockSpec((ROW_TILE, n), lambda i: (i, 0))
    return pl.pallas_call(
        functools.partial(_outproj_kernel, final=final),
        grid=(m // ROW_TILE,),
        in_specs=[_lane_blocks(ysb.shape[0]), _lane_blocks(yfx.shape[0]), _lane_blocks(ym.shape[0]),
                  rows(d), pl.BlockSpec(w.shape, lambda i: (0, 0)),
                  pl.BlockSpec((1, d), lambda i: (0, 0))],
        out_specs=rows(d),
        out_shape=jax.ShapeDtypeStruct((m, d), F32),
        compiler_params=_params("parallel"),
        name="outproj",
    )(ysb, yfx, ym, x2d, w, final_norm_w)


def _split3(x):
    hi = x.astype(BF16)
    r = x - hi.astype(F32)
    mid = r.astype(BF16)
    lo = (r - mid.astype(F32)).astype(BF16)
    return hi, mid, lo


def _fgate_kernel(f_ref, b_ref, o_ref):
    t = f_ref[...] + b_ref[...]
    lf = (jnp.minimum(t, 0.0) - jnp.log(1.0 + jnp.exp(-jnp.abs(t)))) * LOG2E
    n = lf.shape[0]
    r = lax.broadcasted_iota(jnp.int32, (CUM_CHUNK, CUM_CHUNK), 0)
    c = lax.broadcasted_iota(jnp.int32, (CUM_CHUNK, CUM_CHUNK), 1)
    upper = (r <= c).astype(BF16)
    local = sum(_dot(p, upper) for p in _split3(lf))
    totals = jnp.broadcast_to(local[:, CUM_CHUNK - 1:CUM_CHUNK], (n, CUM_CHUNK))
    chunks = n // FOX_HEADS
    gr = lax.broadcasted_iota(jnp.int32, (n, n), 0)
    gc = lax.broadcasted_iota(jnp.int32, (n, n), 1)
    earlier = ((gr // chunks == gc // chunks) & (gc < gr)).astype(BF16)
    offs = sum(_dot(earlier, p) for p in _split3(totals))
    o_ref[...] = local + offs


def _fgate(f_rows, bias_rows, batch):
    n = f_rows.shape[0] // batch
    return pl.pallas_call(
        _fgate_kernel,
        grid=(batch,),
        in_specs=[pl.BlockSpec((n, CUM_CHUNK), lambda b: (b, 0)),
                  pl.BlockSpec((n, 1), lambda b: (0, 0))],
        out_specs=pl.BlockSpec((n, CUM_CHUNK), lambda b: (b, 0)),
        out_shape=jax.ShapeDtypeStruct(f_rows.shape, F32),
        compiler_params=_params("parallel"),
        name="fgate",
    )(f_rows, bias_rows)


def _lane_ids():
    return lax.broadcasted_iota(jnp.int32, (1, LANES), 1)


def _head_lanes(lane, hh):
    return (lane >= hh * HEAD_DIM) & (lane < (hh + 1) * HEAD_DIM)


def _head_queries(q2, lane, hh):
    return jnp.where(_head_lanes(lane, hh), q2, jnp.zeros_like(q2))


def _rep(col):
    return jnp.broadcast_to(col, (col.shape[0], LANES))


def _wide(x):
    return jnp.concatenate([x] * (ATT_TILE // LANES), axis=1)


def _split2(x):
    hi = x.astype(BF16)
    return hi, (x - hi.astype(F32)).astype(BF16)


def _group_mix(lanes_in, group_in, group_out):
    r = lax.broadcasted_iota(jnp.int32, (lanes_in, LANES), 0)
    c = lax.broadcasted_iota(jnp.int32, (lanes_in, LANES), 1)
    return (r // group_in == c // group_out).astype(BF16)


def _group_sum(x, mix):
    hi, lo = _split2(x)
    return _dot(hi, mix) + _dot(lo, mix)


def _head_mix():
    return _group_mix(LANES, HEAD_DIM, HEAD_DIM)


def _finish(o, gate, onw, head_mix, denom=None):
    ms = _group_sum(o * o, head_mix) * (1.0 / HEAD_DIM)
    eps = EPS if denom is None else EPS * (denom * denom)
    yn = o * lax.rsqrt(ms + eps) * onw
    g = gate.astype(F32)
    return (yn * (g * jax.nn.sigmoid(g))).astype(BF16)


def _block_rows(i):
    start = i * ATT_TILE
    return pl.ds(start if isinstance(i, int) else pl.multiple_of(start, ATT_TILE), ATT_TILE)


def _neg_abs(x):
    bits = lax.bitcast_convert_type(x, jnp.uint32) | jnp.uint32(0x80000000)
    return lax.bitcast_convert_type(bits, F32)


def _block_masks():
    t = ATT_TILE
    row = lax.broadcasted_iota(jnp.int32, (t, t), 0)
    col = lax.broadcasted_iota(jnp.int32, (t, t), 1)
    return row, col


def _top_rows(fn, x, fresh_top):
    if not fresh_top:
        return x
    if x.shape[0] == ATT_TILE:
        return fn(x)
    return jnp.concatenate([fn(x[:ATT_TILE]), x[ATT_TILE:]], axis=0)


def _sb_kernel(q_ref, k_ref, v_ref, g_ref, onw_ref, y_ref,
               carry0_ref, carry1_ref, acc0_ref, acc1_ref, alive_ref):
    t = ATT_TILE
    nq = q_ref.shape[1] // t
    assert nq % 2 == 0
    carry_refs = (carry0_ref, carry1_ref)
    acc_refs = (acc0_ref, acc1_ref)
    lane = _lane_ids()
    row, col = _block_masks()
    strict = col < row
    incl = (row >= col).astype(BF16)
    zeros = jnp.zeros((t, LANES), F32)
    heads = range(2)

    rows_of = _block_rows

    def mask_top(u):
        return jnp.where(strict, u, 0.0)

    def qk(hh, q, j):
        return _dot_nt(_head_queries(q, lane, hh), k_ref[0, rows_of(j), :])

    def softplus_parts(z, fresh_top):
        sp = jnp.maximum(z, 0.0) + jnp.log(1.0 + jnp.exp2(-jnp.abs(z))) * LOG2E
        return (_top_rows(mask_top, sp, fresh_top).astype(BF16),)

    def suffix_sums(sp):
        return _dot(sp, incl)

    def weights(z, cs, fresh_top):
        return _top_rows(mask_top, jnp.exp2(z - cs), fresh_top).astype(BF16)

    def pv(a, j):
        return _dot(a, v_ref[0, rows_of(j), :])

    def set_alive(hh, i, carry):
        alive_ref[hh, i] = (jnp.max(carry) > ZERO_WEIGHT_LOG2).astype(jnp.int32)

    def later_key_blocks(i, _):
        def body(jj, _):
            j = i - 2 - jj
            for hh in heads:
                @pl.when(alive_ref[hh, i] != 0)
                def _():
                    z = qk(hh, q_ref[0, rows_of(i), :], j)
                    cs = suffix_sums(*softplus_parts(z, False))
                    carry_in = carry_refs[hh][rows_of(i), :]
                    carry = carry_in - _rep(cs[:, 0:1])
                    acc_refs[hh][rows_of(i), :] += pv(weights(z, cs, False), j) * jnp.exp2(carry_in)
                    carry_refs[hh][rows_of(i), :] = carry
                    set_alive(hh, i, carry)
            return 0

        lax.fori_loop(0, i - 1, body, 0)
        return 0

    for hh in heads:
        carry_refs[hh][rows_of(nq - 1), :] = zeros
        acc_refs[hh][rows_of(nq - 1), :] = zeros

    def key_block_pair(jj):
        j = nq - 1 - 2 * jj
        ib = min(j + 1, nq - 1)
        q_j = q_ref[0, rows_of(j), :]
        qa = jnp.concatenate([q_j, q_ref[0, rows_of(ib), :]], axis=0)
        qb = jnp.concatenate([q_ref[0, rows_of(j - 1), :], q_j], axis=0)
        za = [qk(hh, qa, j) for hh in heads]
        zb = [qk(hh, qb, j - 1) for hh in heads]
        pa = [softplus_parts(z, True) for z in za]
        pb = [softplus_parts(z, True) for z in zb]
        ca = [suffix_sums(*p) for p in pa]
        cb = [suffix_sums(*p) for p in pb]
        pva = [pv(weights(za[hh], ca[hh], True), j) for hh in heads]
        pvb = [pv(weights(zb[hh], cb[hh], True), j - 1) for hh in heads]
        for hh in heads:
            sums_a, sums_b = _rep(ca[hh][:, 0:1]), _rep(cb[hh][:, 0:1])
            carry_ib = carry_refs[hh][rows_of(ib), :]
            carry_j = -sums_a[:t]
            new_ib = carry_ib - sums_a[t:]
            new_j = carry_j - sums_b[t:]
            carry_refs[hh][rows_of(ib), :] = new_ib
            acc_refs[hh][rows_of(ib), :] += pva[hh][t:] * jnp.exp2(carry_ib)
            set_alive(hh, ib, new_ib)
            carry_refs[hh][rows_of(j), :] = new_j
            acc_refs[hh][rows_of(j), :] = pva[hh][:t] + pvb[hh][t:] * jnp.exp2(carry_j)
            set_alive(hh, j, new_j)
            carry_refs[hh][rows_of(j - 1), :] = -sums_b[:t]
            acc_refs[hh][rows_of(j - 1), :] = pvb[hh][:t]

    for jj in range(nq // 2):
        key_block_pair(jj)
    lax.fori_loop(2, nq, later_key_blocks, 0)

    head_mix = _head_mix()

    def finish(i, _):
        rows = rows_of(i)
        o = jnp.where(lane < HEAD_DIM, acc0_ref[rows, :], acc1_ref[rows, :])
        y_ref[0, rows, :] = _finish(o, g_ref[0, rows, :], onw_ref[...], head_mix)
        return 0

    lax.fori_loop(0, nq, finish, 0, unroll=True)


def _fox_kernel(lend_ref, q_ref, k_ref, v_ref, lk_ref, g_ref, onw_ref, y_ref,
                m0_ref, m1_ref, l0_ref, l1_ref, acc0_ref, acc1_ref, qsq_ref, bound_ref):
    t = ATT_TILE
    nq = q_ref.shape[1] // t
    assert nq % 2 == 0
    m_refs, l_refs, acc_refs = (m0_ref, m1_ref), (l0_ref, l1_ref), (acc0_ref, acc1_ref)
    hp = pl.program_id(1)
    heads = range(2)
    gate_row = [pl.program_id(0) * FOX_HEADS + 2 * hp + hh for hh in heads]
    lane = _lane_ids()
    row, col = _block_masks()
    causal = col <= row
    zeros = jnp.zeros((t, LANES), F32)
    rows_of = _block_rows

    ksq = jnp.zeros((t, LANES), BF16)
    for i in range(nq):
        qb = q_ref[0, rows_of(i), :]
        kb = k_ref[0, rows_of(i), :]
        qsq_ref[pl.ds(i, 1), :] = jnp.max(qb * qb, axis=0, keepdims=True).astype(F32)
        ksq = jnp.maximum(ksq, kb * kb)
    ksq = jnp.max(ksq, axis=0, keepdims=True).astype(F32)
    reach = []
    for hh in heads:
        mine = _head_lanes(lane, hh)
        k2 = jnp.sum(jnp.where(mine, ksq, 0.0), axis=-1, keepdims=True)
        q2 = jnp.sum(jnp.where(mine, qsq_ref[...], 0.0), axis=-1, keepdims=True)
        reach.append(jnp.sqrt(q2 * k2) * NORM_INFLATE)

    def qk(hh, q, j):
        return _dot_nt(_head_queries(q, lane, hh), k_ref[0, rows_of(j), :])

    def key_gates(hh, j):
        return lk_ref[0, 2 * hp + hh, pl.ds(j, 1), :]

    def halves(x):
        return x[:, :LANES] + x[:, LANES:]

    def fresh(s):
        s = jnp.where(causal, s, MASKED)
        m = jnp.max(s, axis=-1, keepdims=True)
        p = jnp.exp2(s - m)
        return _rep(m), p, halves(p)

    def update(s, m_old, l_old):
        m_new = jnp.maximum(m_old, _rep(jnp.max(s, axis=-1, keepdims=True)))
        alpha = jnp.exp2(m_old - m_new)
        p = jnp.exp2(s - _wide(m_new))
        return m_new, p, alpha * l_old + halves(p), alpha

    def pv(p, j):
        return _dot(p.astype(BF16), v_ref[0, rows_of(j), :])

    def set_bound(hh, i, m_diag):
        gap = reach[hh][i:i + 1, :] - jnp.min(m_diag, axis=0, keepdims=True)[:, 0:1]
        bound_ref[hh, i] = jnp.max(gap) + lend_ref[gate_row[hh], i] + BOUND_SLACK_LOG2

    def later_key_blocks(i, _):
        def body(jj, _):
            j = i - 2 - jj
            for hh in heads:
                @pl.when(bound_ref[hh, i] - lend_ref[gate_row[hh], j] >= ZERO_WEIGHT_LOG2)
                def _():
                    s = qk(hh, q_ref[0, rows_of(i), :], j)
                    s = s + (lend_ref[gate_row[hh], i] - key_gates(hh, j))
                    m, p, l, alpha = update(s, m_refs[hh][rows_of(i), :], l_refs[hh][rows_of(i), :])
                    m_refs[hh][rows_of(i), :] = m
                    l_refs[hh][rows_of(i), :] = l
                    acc_refs[hh][rows_of(i), :] = alpha * acc_refs[hh][rows_of(i), :] + pv(p, j)
            return 0

        lax.fori_loop(0, i - 1, body, 0)
        return 0

    for hh in heads:
        for ref in (m_refs[hh], l_refs[hh], acc_refs[hh]):
            ref[rows_of(nq - 1), :] = zeros

    def key_block_pair(jj):
        j = nq - 1 - 2 * jj
        ib = min(j + 1, nq - 1)
        q_j = q_ref[0, rows_of(j), :]
        qa = jnp.concatenate([q_j, q_ref[0, rows_of(ib), :]], axis=0)
        qb = jnp.concatenate([q_ref[0, rows_of(j - 1), :], q_j], axis=0)
        za = [qk(hh, qa, j) for hh in heads]
        zb = [qk(hh, qb, j - 1) for hh in heads]
        wa, wb, new = [], [], []
        for hh in heads:
            g = gate_row[hh]
            gates_a, gates_b = key_gates(hh, j), key_gates(hh, j - 1)
            m_a, p_a, l_a = fresh(za[hh][:t] + (lend_ref[g, j] - gates_a))
            m_ib, p_ib, l_ib, alpha_ib = update(za[hh][t:] + (lend_ref[g, ib] - gates_a),
                                                m_refs[hh][rows_of(ib), :], l_refs[hh][rows_of(ib), :])
            m_b, p_b, l_b = fresh(zb[hh][:t] + (lend_ref[g, j - 1] - gates_b))
            m_j, p_j, l_j, alpha_j = update(zb[hh][t:] + (lend_ref[g, j] - gates_b), m_a, l_a)
            wa.append(jnp.concatenate([p_a, p_ib], axis=0))
            wb.append(jnp.concatenate([p_b, p_j], axis=0))
            new.append((m_a, m_ib, l_ib, alpha_ib, m_b, l_b, m_j, l_j, alpha_j))
        pva = [pv(w, j) for w in wa]
        pvb = [pv(w, j - 1) for w in wb]
        for hh in heads:
            m_a, m_ib, l_ib, alpha_ib, m_b, l_b, m_j, l_j, alpha_j = new[hh]
            m_refs[hh][rows_of(ib), :] = m_ib
            l_refs[hh][rows_of(ib), :] = l_ib
            acc_refs[hh][rows_of(ib), :] = alpha_ib * acc_refs[hh][rows_of(ib), :] + pva[hh][t:]
            m_refs[hh][rows_of(j), :] = m_j
            l_refs[hh][rows_of(j), :] = l_j
            acc_refs[hh][rows_of(j), :] = alpha_j * pva[hh][:t] + pvb[hh][t:]
            set_bound(hh, j, m_a)
            m_refs[hh][rows_of(j - 1), :] = m_b
            l_refs[hh][rows_of(j - 1), :] = l_b
            acc_refs[hh][rows_of(j - 1), :] = pvb[hh][:t]
            set_bound(hh, j - 1, m_b)

    for jj in range(nq // 2):
        key_block_pair(jj)
    lax.fori_loop(2, nq, later_key_blocks, 0)

    head_mix = _head_mix()
    pair_mix = _group_mix(2 * LANES, LANES, HEAD_DIM)

    def finish(i, _):
        rows = rows_of(i)
        partial = jnp.concatenate([l0_ref[rows, :], l1_ref[rows, :]], axis=1)
        acc = jnp.where(lane < HEAD_DIM, acc0_ref[rows, :], acc1_ref[rows, :])
        denom = _dot(partial.astype(BF16), pair_mix)
        y_ref[0, rows, :] = _finish(acc, g_ref[0, rows, :], onw_ref[...], head_mix, denom)
        return 0

    lax.fori_loop(0, nq, finish, 0, unroll=True)


def _mem_kernel(q_ref, k_ref, v_ref, g_ref, onw_ref, y_ref):
    q2 = q_ref[0]
    k2 = k_ref[0]
    v2 = v_ref[0]
    lane = _lane_ids()
    outs, sums = [], []
    for hh in range(2):
        s = _dot_nt(_head_queries(q2, lane, hh), k2)
        p = jnp.exp2(s - jnp.max(s, axis=-1, keepdims=True))
        sums.append(jnp.sum(p, axis=-1, keepdims=True))
        outs.append(_dot(p.astype(BF16), v2))
    first = lane < HEAD_DIM
    o = jnp.where(first, outs[0], outs[1])
    denom = jnp.where(first, sums[0], sums[1])
    y_ref[0] = _finish(o, g_ref[0], onw_ref[...], _head_mix(), denom)


def _seq_block(seq, first_block):
    return pl.BlockSpec((1, seq, LANES), lambda b, hp, *_: (first_block + hp, b, 0))


def _onw_block(first_block):
    return pl.BlockSpec((1, LANES), lambda b, hp, *_: (0, first_block + hp))


def _sb_attention(qkv, gate, onw, batch, q_blk, k_blk, v_blk, g_blk):
    seq = qkv.shape[1] // batch
    nq = seq // ATT_TILE
    state = pltpu.VMEM((seq, LANES), F32)
    return pl.pallas_call(
        _sb_kernel,
        grid=(batch, SB_WIDTH // LANES),
        in_specs=[_seq_block(seq, q_blk), _seq_block(seq, k_blk), _seq_block(seq, v_blk),
                  _seq_block(seq, g_blk), _onw_block(g_blk)],
        out_specs=_seq_block(seq, 0),
        out_shape=jax.ShapeDtypeStruct((SB_WIDTH // LANES, batch * seq, LANES), BF16),
        scratch_shapes=[state, state, state, state, pltpu.SMEM((2, nq), jnp.int32)],
        compiler_params=_params("parallel", "parallel"),
        name="sb_attention",
    )(qkv, qkv, qkv, gate, onw)


def _fox_attention(lend, qkv, lk, gate, onw, batch, q_blk, k_blk, v_blk, g_blk):
    seq = qkv.shape[1] // batch
    nq = seq // ATT_TILE
    state = pltpu.VMEM((seq, LANES), F32)
    grid_spec = pltpu.PrefetchScalarGridSpec(
        num_scalar_prefetch=1,
        grid=(batch, FOX_WIDTH // LANES),
        in_specs=[_seq_block(seq, q_blk), _seq_block(seq, k_blk), _seq_block(seq, v_blk),
                  pl.BlockSpec((1, FOX_HEADS, nq, ATT_TILE), lambda b, hp, *_: (b, 0, 0, 0)),
                  _seq_block(seq, g_blk), _onw_block(g_blk)],
        out_specs=_seq_block(seq, 0),
        scratch_shapes=[state] * 6 + [pltpu.VMEM((nq, LANES), F32), pltpu.SMEM((2, nq), F32)],
    )
    return pl.pallas_call(
        _fox_kernel,
        grid_spec=grid_spec,
        out_shape=jax.ShapeDtypeStruct((FOX_WIDTH // LANES, batch * seq, LANES), BF16),
        compiler_params=_params("parallel", "parallel"),
        name="fox_attention",
    )(lend, qkv, qkv, qkv, lk, gate, onw)


def _mem_attention(mq, mem_kv, gate, onw, batch, g_blk):
    seq = mq.shape[1] // batch
    mem_len = mem_kv.shape[1] // batch
    steps = seq // MEM_Q_TILE
    tile = lambda first: pl.BlockSpec((1, MEM_Q_TILE, LANES),
                                      lambda b, hp, i: (first + hp, b * steps + i, 0))
    kv = lambda first: pl.BlockSpec((1, mem_len, LANES), lambda b, hp, i: (first + hp, b, 0))
    return pl.pallas_call(
        _mem_kernel,
        grid=(batch, MEM_WIDTH // LANES, steps),
        in_specs=[tile(0), kv(0), kv(MEM_WIDTH // LANES), tile(g_blk),
                  pl.BlockSpec((1, LANES), lambda b, hp, i: (0, g_blk + hp))],
        out_specs=tile(0),
        out_shape=jax.ShapeDtypeStruct((MEM_WIDTH // LANES, batch * seq, LANES), BF16),
        compiler_params=_params("parallel", "parallel", "arbitrary"),
        name="mem_attention",
    )(mq, mem_kv, mem_kv, gate, onw)


def kernel(x, mem, norm_w, w_in, b_forget, mem_norm_w, w_mem_kv, out_norm_w, w_out, final_norm_w):
    batch, seq, d = x.shape
    mem_len = mem.shape[1]
    depth = w_in.shape[0]
    m = batch * seq
    nblk = seq // ATT_TILE
    chunks = seq // CUM_CHUNK
    blocks = lambda width: width // LANES
    sb_q, sb_k, sb_v = 0, blocks(SB_WIDTH), 2 * blocks(SB_WIDTH)
    fx_q = 3 * blocks(SB_WIDTH)
    fx_k, fx_v = fx_q + blocks(FOX_WIDTH), fx_q + 2 * blocks(FOX_WIDTH)
    g_sb, g_fx, g_m = 0, blocks(SB_WIDTH), blocks(SB_WIDTH + FOX_WIDTH)

    ones = jnp.ones((SB_WIDTH,), F32)
    col_scale = jnp.concatenate([ones * Q_SCALE, ones, ones, ones * Q_SCALE, ones, ones])[None, :]
    off_f = QKV_WIDTH
    off_mq = off_f + FOX_HEADS
    off_g = off_mq + MEM_WIDTH

    x2d = x.reshape(m, d)
    mem2d = mem.reshape(batch * mem_len, d)
    for layer in range(depth):
        w = w_in[layer]
        wqkv = w[:, :QKV_WIDTH].astype(BF16)
        wf = jnp.pad(w[:, off_f:off_mq], ((0, 0), (0, LANES - FOX_HEADS))).astype(BF16)
        wmq = w[:, off_mq:off_g].astype(BF16)
        wg = w[:, off_g:].astype(BF16)
        qkv, mq, gate, f = _inproj(x2d, norm_w[layer][None, :], wqkv, wmq, wg, wf, col_scale)
        onw = out_norm_w[layer][None, :]

        f_rows = f[:, :FOX_HEADS].reshape(batch, seq, FOX_HEADS).transpose(0, 2, 1)
        f_rows = f_rows.reshape(batch * FOX_HEADS * chunks, CUM_CHUNK)
        bias_rows = jnp.repeat(b_forget[layer], chunks)[:, None]
        lk = _fgate(f_rows, bias_rows, batch).reshape(batch, FOX_HEADS, nblk, ATT_TILE)
        lend = lk[:, :, :, ATT_TILE - 1].reshape(batch * FOX_HEADS, nblk)

        mem_kv = _memkv(mem2d, mem_norm_w[layer][None, :], w_mem_kv[layer].astype(BF16))

        y_sb = _sb_attention(qkv, gate, onw, batch, sb_q, sb_k, sb_v, g_sb)
        y_fx = _fox_attention(lend, qkv, lk, gate, onw, batch, fx_q, fx_k, fx_v, g_fx)
        y_m = _mem_attention(mq, mem_kv, gate, onw, batch, g_m)

        x2d = _outproj(y_sb, y_fx, y_m, x2d, w_out[layer].astype(BF16),
                       final_norm_w[None, :], layer == depth - 1)
    return x2d.reshape(batch, seq, d)
```

```python
import functools

import jax
import jax.numpy as jnp
from jax import lax
from jax.experimental import pallas as pl
from jax.experimental.pallas import tpu as pltpu

HEAD_DIM = 64
SB_HEADS = 8
FOX_HEADS = 8
MEM_HEADS = 4
SB_WIDTH = SB_HEADS * HEAD_DIM
FOX_WIDTH = FOX_HEADS * HEAD_DIM
MEM_WIDTH = MEM_HEADS * HEAD_DIM
MIX_WIDTH = SB_WIDTH + FOX_WIDTH + MEM_WIDTH
QKV_WIDTH = 3 * SB_WIDTH + 3 * FOX_WIDTH
EPS = 1e-6
LOG2E = 1.4426950408889634
Q_SCALE = HEAD_DIM ** -0.5 * LOG2E
MASKED = -1e30
ZERO_WEIGHT_LOG2 = -150.0
BOUND_SLACK_LOG2 = 2.0
NORM_INFLATE = 1.0 + 2.0 ** -7

LANES = 128
ROW_TILE = 512
COL_CHUNK = 512
ATT_TILE = 256
MEM_Q_TILE = 512
CUM_CHUNK = 128
VMEM_LIMIT = 56 * 1024 * 1024

F32 = jnp.float32
BF16 = jnp.bfloat16
NT_DIMS = (((1,), (1,)), ((), ()))


def _params(*semantics):
    return pltpu.CompilerParams(dimension_semantics=semantics, vmem_limit_bytes=VMEM_LIMIT)


def _rmsnorm(x, w):
    return x * lax.rsqrt(jnp.mean(x * x, axis=-1, keepdims=True) + EPS) * w


def _dot(a, b):
    return jnp.dot(a, b, preferred_element_type=F32)


def _dot_nt(a, b):
    return lax.dot_general(a, b, NT_DIMS, preferred_element_type=F32)


def _store_lane_blocks(ref, first_block, values):
    for c in range(values.shape[1] // LANES):
        ref[first_block + c] = values[:, c * LANES:(c + 1) * LANES]


def _lane_blocks(n_blocks):
    return pl.BlockSpec((n_blocks, ROW_TILE, LANES), lambda i: (0, i, 0))


def _inproj_kernel(x_ref, nw_ref, wqkv_ref, wmq_ref, wg_ref, wf_ref, cs_ref,
                   qkv_ref, mq_ref, gate_ref, f_ref):
    hb = _rmsnorm(x_ref[...], nw_ref[...]).astype(BF16)
    for c in range(0, QKV_WIDTH, COL_CHUNK):
        sl = slice(c, c + COL_CHUNK)
        _store_lane_blocks(qkv_ref, c // LANES,
                           (_dot(hb, wqkv_ref[:, sl]) * cs_ref[:, sl]).astype(BF16))
    _store_lane_blocks(mq_ref, 0, (_dot(hb, wmq_ref[...]) * Q_SCALE).astype(BF16))
    for c in range(0, MIX_WIDTH, 256):
        _store_lane_blocks(gate_ref, c // LANES, _dot(hb, wg_ref[:, c:c + 256]))
    f_ref[...] = _dot(hb, wf_ref[...])


def _inproj(x2d, norm_w, wqkv, wmq, wg, wf, col_scale):
    m, d = x2d.shape
    full = lambda shape: pl.BlockSpec(shape, lambda i: (0, 0))
    rows = lambda n: pl.BlockSpec((ROW_TILE, n), lambda i: (i, 0))
    blocks = lambda width: width // LANES
    return pl.pallas_call(
        _inproj_kernel,
        grid=(m // ROW_TILE,),
        in_specs=[rows(d), full((1, d)), full(wqkv.shape), full(wmq.shape), full(wg.shape),
                  full(wf.shape), full((1, QKV_WIDTH))],
        out_specs=[_lane_blocks(blocks(QKV_WIDTH)), _lane_blocks(blocks(MEM_WIDTH)),
                   _lane_blocks(blocks(MIX_WIDTH)), rows(LANES)],
        out_shape=[jax.ShapeDtypeStruct((blocks(QKV_WIDTH), m, LANES), BF16),
                   jax.ShapeDtypeStruct((blocks(MEM_WIDTH), m, LANES), BF16),
                   jax.ShapeDtypeStruct((blocks(MIX_WIDTH), m, LANES), F32),
                   jax.ShapeDtypeStruct((m, LANES), F32)],
        compiler_params=_params("parallel"),
        name="inproj",
    )(x2d, norm_w, wqkv, wmq, wg, wf, col_scale)


def _memkv_kernel(m_ref, nw_ref, w_ref, o_ref):
    hb = _rmsnorm(m_ref[...], nw_ref[...]).astype(BF16)
    _store_lane_blocks(o_ref, 0, _dot(hb, w_ref[...]).astype(BF16))


def _memkv(mem2d, norm_w, w):
    m, d = mem2d.shape
    n = w.shape[1]
    return pl.pallas_call(
        _memkv_kernel,
        grid=(m // ROW_TILE,),
        in_specs=[pl.BlockSpec((ROW_TILE, d), lambda i: (i, 0)),
                  pl.BlockSpec((1, d), lambda i: (0, 0)),
                  pl.BlockSpec((d, n), lambda i: (0, 0))],
        out_specs=_lane_blocks(n // LANES),
        out_shape=jax.ShapeDtypeStruct((n // LANES, m, LANES), BF16),
        compiler_params=_params("parallel"),
        name="memkv",
    )(mem2d, norm_w, w)


def _outproj_kernel(ysb_ref, yfx_ref, ym_ref, x_ref, w_ref, fnw_ref, o_ref, *, final):
    y = jnp.concatenate([ref[c] for ref in (ysb_ref, yfx_ref, ym_ref) for c in range(ref.shape[0])],
                        axis=1)
    acc = x_ref[...] + _dot(y, w_ref[...])
    if final:
        acc = _rmsnorm(acc, fnw_ref[...])
    o_ref[...] = acc


def _outproj(ysb, yfx, ym, x2d, w, final_norm_w, final):
    m, d = x2d.shape
    rows = lambda n: pl.BlockSpec((ROW_TILE, n), lambda i: (i, 0))
    return pl.pallas_call(
        functools.partial(_outproj_kernel, final=final),
        grid=(m // ROW_TILE,),
        in_specs=[_lane_blocks(ysb.shape[0]), _lane_blocks(yfx.shape[0]), _lane_blocks(ym.shape[0]),
                  rows(d), pl.BlockSpec(w.shape, lambda i: (0, 0)),
                  pl.BlockSpec((1, d), lambda i: (0, 0))],
        out_specs=rows(d),
        out_shape=jax.ShapeDtypeStruct((m, d), F32),
        compiler_params=_params("parallel"),
        name="outproj",
    )(ysb, yfx, ym, x2d, w, final_norm_w)


def _split3(x):
    hi = x.astype(BF16)
    r = x - hi.astype(F32)
    mid = r.astype(BF16)
    lo = (r - mid.astype(F32)).astype(BF16)
    return hi, mid, lo


def _fgate_kernel(f_ref, b_ref, o_ref):
    t = f_ref[...] + b_ref[...]
    lf = (jnp.minimum(t, 0.0) - jnp.log(1.0 + jnp.exp(-jnp.abs(t)))) * LOG2E
    n = lf.shape[0]
    r = lax.broadcasted_iota(jnp.int32, (CUM_CHUNK, CUM_CHUNK), 0)
    c = lax.broadcasted_iota(jnp.int32, (CUM_CHUNK, CUM_CHUNK), 1)
    upper = (r <= c).astype(BF16)
    local = sum(_dot(p, upper) for p in _split3(lf))
    totals = jnp.broadcast_to(local[:, CUM_CHUNK - 1:CUM_CHUNK], (n, CUM_CHUNK))
    chunks = n // FOX_HEADS
    gr = lax.broadcasted_iota(jnp.int32, (n, n), 0)
    gc = lax.broadcasted_iota(jnp.int32, (n, n), 1)
    earlier = ((gr // chunks == gc // chunks) & (gc < gr)).astype(BF16)
    offs = sum(_dot(earlier, p) for p in _split3(totals))
    o_ref[...] = local + offs


def _fgate(f_rows, bias_rows, batch):
    n = f_rows.shape[0] // batch
    return pl.pallas_call(
        _fgate_kernel,
        grid=(batch,),
        in_specs=[pl.BlockSpec((n, CUM_CHUNK), lambda b: (b, 0)),
                  pl.BlockSpec((n, 1), lambda b: (0, 0))],
        out_specs=pl.BlockSpec((n, CUM_CHUNK), lambda b: (b, 0)),
        out_shape=jax.ShapeDtypeStruct(f_rows.shape, F32),
        compiler_params=_params("parallel"),
        name="fgate",
    )(f_rows, bias_rows)


def _lane_ids():
    return lax.broadcasted_iota(jnp.int32, (1, LANES), 1)


def _head_lanes(lane, hh):
    return (lane >= hh * HEAD_DIM) & (lane < (hh + 1) * HEAD_DIM)


def _head_queries(q2, lane, hh):
    return jnp.where(_head_lanes(lane, hh), q2, jnp.zeros_like(q2))


def _rep(col):
    return jnp.broadcast_to(col, (col.shape[0], LANES))


def _wide(x):
    return jnp.concatenate([x] * (ATT_TILE // LANES), axis=1)


def _split2(x):
    hi = x.astype(BF16)
    return hi, (x - hi.astype(F32)).astype(BF16)


def _group_mix(lanes_in, group_in, group_out):
    r = lax.broadcasted_iota(jnp.int32, (lanes_in, LANES), 0)
    c = lax.broadcasted_iota(jnp.int32, (lanes_in, LANES), 1)
    return (r // group_in == c // group_out).astype(BF16)


def _group_sum(x, mix):
    hi, lo = _split2(x)
    return _dot(hi, mix) + _dot(lo, mix)


def _head_mix():
    return _group_mix(LANES, HEAD_DIM, HEAD_DIM)


def _finish(o, gate, onw, head_mix, denom=None):
    ms = _group_sum(o * o, head_mix) * (1.0 / HEAD_DIM)
    eps = EPS if denom is None else EPS * (denom * denom)
    yn = o * lax.rsqrt(ms + eps) * onw
    g = gate.astype(F32)
    return (yn * (g * jax.nn.sigmoid(g))).astype(BF16)


def _block_rows(i):
    start = i * ATT_TILE
    return pl.ds(start if isinstance(i, int) else pl.multiple_of(start, ATT_TILE), ATT_TILE)


def _neg_abs(x):
    bits = lax.bitcast_convert_type(x, jnp.uint32) | jnp.uint32(0x80000000)
    return lax.bitcast_convert_type(bits, F32)


def _block_masks():
    t = ATT_TILE
    row = lax.broadcasted_iota(jnp.int32, (t, t), 0)
    col = lax.broadcasted_iota(jnp.int32, (t, t), 1)
    return row, col


def _top_rows(fn, x, fresh_top):
    if not fresh_top:
        return x
    if x.shape[0] == ATT_TILE:
        return fn(x)
    return jnp.concatenate([fn(x[:ATT_TILE]), x[ATT_TILE:]], axis=0)


def _sb_kernel(q_ref, k_ref, v_ref, g_ref, onw_ref, y_ref,
               carry0_ref, carry1_ref, acc0_ref, acc1_ref, alive_ref):
    t = ATT_TILE
    nq = q_ref.shape[1] // t
    assert nq % 2 == 0
    carry_refs = (carry0_ref, carry1_ref)
    acc_refs = (acc0_ref, acc1_ref)
    lane = _lane_ids()
    row, col = _block_masks()
    strict = col < row
    incl = (row >= col).astype(BF16)
    zeros = jnp.zeros((t, LANES), F32)
    heads = range(2)

    rows_of = _block_rows

    def mask_top(u):
        return jnp.where(strict, u, 0.0)

    def qk(hh, q, j):
        return _dot_nt(_head_queries(q, lane, hh), k_ref[0, rows_of(j), :])

    def softplus_parts(z, fresh_top):
        sp = jnp.maximum(z, 0.0) + jnp.log(1.0 + jnp.exp2(-jnp.abs(z))) * LOG2E
        return (_top_rows(mask_top, sp, fresh_top).astype(BF16),)

    def suffix_sums(sp):
        return _dot(sp, incl)

    def weights(z, cs, fresh_top):
        return _top_rows(mask_top, jnp.exp2(z - cs), fresh_top).astype(BF16)

    def pv(a, j):
        return _dot(a, v_ref[0, rows_of(j), :])

    def set_alive(hh, i, carry):
        alive_ref[hh, i] = (jnp.max(carry) > ZERO_WEIGHT_LOG2).astype(jnp.int32)

    def later_key_blocks(i, _):
        def body(jj, _):
            j = i - 2 - jj
            for hh in heads:
                @pl.when(alive_ref[hh, i] != 0)
                def _():
                    z = qk(hh, q_ref[0, rows_of(i), :], j)
                    cs = suffix_sums(*softplus_parts(z, False))
                    carry_in = carry_refs[hh][rows_of(i), :]
                    carry = carry_in - _rep(cs[:, 0:1])
                    acc_refs[hh][rows_of(i), :] += pv(weights(z, cs, False), j) * jnp.exp2(carry_in)
                    carry_refs[hh][rows_of(i), :] = carry
                    set_alive(hh, i, carry)
            return 0

        lax.fori_loop(0, i - 1, body, 0)
        return 0

    for hh in heads:
        carry_refs[hh][rows_of(nq - 1), :] = zeros
        acc_refs[hh][rows_of(nq - 1), :] = zeros

    def key_block_pair(jj):
        j = nq - 1 - 2 * jj
        ib = min(j + 1, nq - 1)
        q_j = q_ref[0, rows_of(j), :]
        qa = jnp.concatenate([q_j, q_ref[0, rows_of(ib), :]], axis=0)
        qb = jnp.concatenate([q_ref[0, rows_of(j - 1), :], q_j], axis=0)
        za = [qk(hh, qa, j) for hh in heads]
        zb = [qk(hh, qb, j - 1) for hh in heads]
        pa = [softplus_parts(z, True) for z in za]
        pb = [softplus_parts(z, True) for z in zb]
        ca = [suffix_sums(*p) for p in pa]
        cb = [suffix_sums(*p) for p in pb]
        pva = [pv(weights(za[hh], ca[hh], True), j) for hh in heads]
        pvb = [pv(weights(zb[hh], cb[hh], True), j - 1) for hh in heads]
        for hh in heads:
            sums_a, sums_b = _rep(ca[hh][:, 0:1]), _rep(cb[hh][:, 0:1])
            carry_ib = carry_refs[hh][rows_of(ib), :]
            carry_j = -sums_a[:t]
            new_ib = carry_ib - sums_a[t:]
            new_j = carry_j - sums_b[t:]
            carry_refs[hh][rows_of(ib), :] = new_ib
            acc_refs[hh][rows_of(ib), :] += pva[hh][t:] * jnp.exp2(carry_ib)
            set_alive(hh, ib, new_ib)
            carry_refs[hh][rows_of(j), :] = new_j
            acc_refs[hh][rows_of(j), :] = pva[hh][:t] + pvb[hh][t:] * jnp.exp2(carry_j)
            set_alive(hh, j, new_j)
            carry_refs[hh][rows_of(j - 1), :] = -sums_b[:t]
            acc_refs[hh][rows_of(j - 1), :] = pvb[hh][:t]

    for jj in range(nq // 2):
        key_block_pair(jj)
    lax.fori_loop(2, nq, later_key_blocks, 0)

    head_mix = _head_mix()

    def finish(i, _):
        rows = rows_of(i)
        o = jnp.where(lane < HEAD_DIM, acc0_ref[rows, :], acc1_ref[rows, :])
        y_ref[0, rows, :] = _finish(o, g_ref[0, rows, :], onw_ref[...], head_mix)
        return 0

    lax.fori_loop(0, nq, finish, 0, unroll=True)


def _fox_kernel(lend_ref, q_ref, k_ref, v_ref, lk_ref, g_ref, onw_ref, y_ref,
                m0_ref, m1_ref, l0_ref, l1_ref, acc0_ref, acc1_ref, qsq_ref, bound_ref):
    t = ATT_TILE
    nq = q_ref.shape[1] // t
    assert nq % 2 == 0
    m_refs, l_refs, acc_refs = (m0_ref, m1_ref), (l0_ref, l1_ref), (acc0_ref, acc1_ref)
    hp = pl.program_id(1)
    heads = range(2)
    gate_row = [pl.program_id(0) * FOX_HEADS + 2 * hp + hh for hh in heads]
    lane = _lane_ids()
    row, col = _block_masks()
    causal = col <= row
    zeros = jnp.zeros((t, LANES), F32)
    rows_of = _block_rows

    head_sum = _group_mix(LANES, HEAD_DIM, LANES)
    head_sum = jnp.concatenate([head_sum, 1 - head_sum], axis=1)
    ksq = jnp.zeros((t, 2 * LANES), F32)
    for i in range(nq):
        qb = q_ref[0, rows_of(i), :]
        kb = k_ref[0, rows_of(i), :]
        qsq_ref[pl.ds(i, 1), :] = jnp.max(_dot(qb * qb, head_sum), axis=0, keepdims=True)
        ksq = jnp.maximum(ksq, _dot(kb * kb, head_sum))
    ksq = jnp.max(ksq, axis=0, keepdims=True)
    reach = [jnp.sqrt(qsq_ref[:, hh * LANES:hh * LANES + 1] * ksq[:, hh * LANES:hh * LANES + 1])
             * NORM_INFLATE for hh in heads]

    def qk(hh, q, j):
        return _dot_nt(_head_queries(q, lane, hh), k_ref[0, rows_of(j), :])

    def key_gates(hh, j):
        return lk_ref[0, 2 * hp + hh, pl.ds(j, 1), :]

    def halves(x):
        return x[:, :LANES] + x[:, LANES:]

    def fresh(s):
        s = jnp.where(causal, s, MASKED)
        m = jnp.max(s, axis=-1, keepdims=True)
        p = jnp.exp2(s - m)
        return _rep(m), p, halves(p)

    def update(s, m_old, l_old):
        m_new = jnp.maximum(m_old, _rep(jnp.max(s, axis=-1, keepdims=True)))
        alpha = jnp.exp2(m_old - m_new)
        p = jnp.exp2(s - _wide(m_new))
        return m_new, p, alpha * l_old + halves(p), alpha

    def pv(p, j):
        return _dot(p.astype(BF16), v_ref[0, rows_of(j), :])

    def set_bound(hh, i, m_diag):
        gap = reach[hh][i:i + 1, :] - jnp.min(m_diag, axis=0, keepdims=True)[:, 0:1]
        bound_ref[hh, i] = jnp.max(gap) + lend_ref[gate_row[hh], i] + BOUND_SLACK_LOG2

    def later_key_blocks(i, _):
        def body(jj, _):
            j = i - 2 - jj
            for hh in heads:
                @pl.when(bound_ref[hh, i] - lend_ref[gate_row[hh], j] >= ZERO_WEIGHT_LOG2)
                def _():
                    s = qk(hh, q_ref[0, rows_of(i), :], j)
                    s = s + (lend_ref[gate_row[hh], i] - key_gates(hh, j))
                    m, p, l, alpha = update(s, m_refs[hh][rows_of(i), :], l_refs[hh][rows_of(i), :])
                    m_refs[hh][rows_of(i), :] = m
                    l_refs[hh][rows_of(i), :] = l
                    acc_refs[hh][rows_of(i), :] = alpha * acc_refs[hh][rows_of(i), :] + pv(p, j)
            return 0

        lax.fori_loop(0, i - 1, body, 0)
        return 0

    for hh in heads:
        for ref in (m_refs[hh], l_refs[hh], acc_refs[hh]):
            ref[rows_of(nq - 1), :] = zeros

    def key_block_pair(jj):
        j = nq - 1 - 2 * jj
        ib = min(j + 1, nq - 1)
        q_j = q_ref[0, rows_of(j), :]
        qa = jnp.concatenate([q_j, q_ref[0, rows_of(ib), :]], axis=0)
        qb = jnp.concatenate([q_ref[0, rows_of(j - 1), :], q_j], axis=0)
        za = [qk(hh, qa, j) for hh in heads]
        zb = [qk(hh, qb, j - 1) for hh in heads]
        wa, wb, new = [], [], []
        for hh in heads:
            g = gate_row[hh]
            gates_a, gates_b = key_gates(hh, j), key_gates(hh, j - 1)
            m_a, p_a, l_a = fresh(za[hh][:t] + (lend_ref[g, j] - gates_a))
            m_ib, p_ib, l_ib, alpha_ib = update(za[hh][t:] + (lend_ref[g, ib] - gates_a),
                                                m_refs[hh][rows_of(ib), :], l_refs[hh][rows_of(ib), :])
            m_b, p_b, l_b = fresh(zb[hh][:t] + (lend_ref[g, j - 1] - gates_b))
            m_j, p_j, l_j, alpha_j = update(zb[hh][t:] + (lend_ref[g, j] - gates_b), m_a, l_a)
            wa.append(jnp.concatenate([p_a, p_ib], axis=0))
            wb.append(jnp.concatenate([p_b, p_j], axis=0))
            new.append((m_a, m_ib, l_ib, alpha_ib, m_b, l_b, m_j, l_j, alpha_j))
        pva = [pv(w, j) for w in wa]
        pvb = [pv(w, j - 1) for w in wb]
        for hh in heads:
            m_a, m_ib, l_ib, alpha_ib, m_b, l_b, m_j, l_j, alpha_j = new[hh]
            m_refs[hh][rows_of(ib), :] = m_ib
            l_refs[hh][rows_of(ib), :] = l_ib
            acc_refs[hh][rows_of(ib), :] = alpha_ib * acc_refs[hh][rows_of(ib), :] + pva[hh][t:]
            m_refs[hh][rows_of(j), :] = m_j
            l_refs[hh][rows_of(j), :] = l_j
            acc_refs[hh][rows_of(j), :] = alpha_j * pva[hh][:t] + pvb[hh][t:]
            set_bound(hh, j, m_a)
            m_refs[hh][rows_of(j - 1), :] = m_b
            l_refs[hh][rows_of(j - 1), :] = l_b
            acc_refs[hh][rows_of(j - 1), :] = pvb[hh][:t]
            set_bound(hh, j - 1, m_b)

    for jj in range(nq // 2):
        key_block_pair(jj)
    lax.fori_loop(2, nq, later_key_blocks, 0)

    head_mix = _head_mix()
    pair_mix = _group_mix(2 * LANES, LANES, HEAD_DIM)

    def finish(i, _):
        rows = rows_of(i)
        partial = jnp.concatenate([l0_ref[rows, :], l1_ref[rows, :]], axis=1)
        acc = jnp.where(lane < HEAD_DIM, acc0_ref[rows, :], acc1_ref[rows, :])
        denom = _dot(partial.astype(BF16), pair_mix)
        y_ref[0, rows, :] = _finish(acc, g_ref[0, rows, :], onw_ref[...], head_mix, denom)
        return 0

    lax.fori_loop(0, nq, finish, 0, unroll=True)


def _mem_kernel(q_ref, k_ref, v_ref, g_ref, onw_ref, y_ref):
    k2 = k_ref[0]
    v2 = v_ref[0]
    lane = _lane_ids()
    first = lane < HEAD_DIM
    head_mix = _head_mix()
    tiles = [pl.ds(r, MEM_Q_TILE) for r in range(0, q_ref.shape[1], MEM_Q_TILE)]
    chains = [(rows, hh) for rows in tiles for hh in range(2)]
    scores = [_dot_nt(_head_queries(q_ref[0, rows, :], lane, hh), k2) for rows, hh in chains]
    probs = [jnp.exp2(s - jnp.max(s, axis=-1, keepdims=True)) for s in scores]
    sums = [jnp.sum(p, axis=-1, keepdims=True) for p in probs]
    outs = [_dot(p.astype(BF16), v2) for p in probs]
    for n, rows in enumerate(tiles):
        o = jnp.where(first, outs[2 * n], outs[2 * n + 1])
        denom = jnp.where(first, sums[2 * n], sums[2 * n + 1])
        y_ref[0, rows, :] = _finish(o, g_ref[0, rows, :], onw_ref[...], head_mix, denom)


def _seq_block(seq, first_block):
    return pl.BlockSpec((1, seq, LANES), lambda b, hp, *_: (first_block + hp, b, 0))


def _onw_block(first_block):
    return pl.BlockSpec((1, LANES), lambda b, hp, *_: (0, first_block + hp))


def _sb_attention(qkv, gate, onw, batch, q_blk, k_blk, v_blk, g_blk):
    seq = qkv.shape[1] // batch
    nq = seq // ATT_TILE
    state = pltpu.VMEM((seq, LANES), F32)
    return pl.pallas_call(
        _sb_kernel,
        grid=(batch, SB_WIDTH // LANES),
        in_specs=[_seq_block(seq, q_blk), _seq_block(seq, k_blk), _seq_block(seq, v_blk),
                  _seq_block(seq, g_blk), _onw_block(g_blk)],
        out_specs=_seq_block(seq, 0),
        out_shape=jax.ShapeDtypeStruct((SB_WIDTH // LANES, batch * seq, LANES), BF16),
        scratch_shapes=[state, state, state, state, pltpu.SMEM((2, nq), jnp.int32)],
        compiler_params=_params("parallel", "parallel"),
        name="sb_attention",
    )(qkv, qkv, qkv, gate, onw)


def _fox_attention(lend, qkv, lk, gate, onw, batch, q_blk, k_blk, v_blk, g_blk):
    seq = qkv.shape[1] // batch
    nq = seq // ATT_TILE
    state = pltpu.VMEM((seq, LANES), F32)
    grid_spec = pltpu.PrefetchScalarGridSpec(
        num_scalar_prefetch=1,
        grid=(batch, FOX_WIDTH // LANES),
        in_specs=[_seq_block(seq, q_blk), _seq_block(seq, k_blk), _seq_block(seq, v_blk),
                  pl.BlockSpec((1, FOX_HEADS, nq, ATT_TILE), lambda b, hp, *_: (b, 0, 0, 0)),
                  _seq_block(seq, g_blk), _onw_block(g_blk)],
        out_specs=_seq_block(seq, 0),
        scratch_shapes=[state] * 6 + [pltpu.VMEM((nq, 2 * LANES), F32), pltpu.SMEM((2, nq), F32)],
    )
    return pl.pallas_call(
        _fox_kernel,
        grid_spec=grid_spec,
        out_shape=jax.ShapeDtypeStruct((FOX_WIDTH // LANES, batch * seq, LANES), BF16),
        compiler_params=_params("parallel", "parallel"),
        name="fox_attention",
    )(lend, qkv, qkv, qkv, lk, gate, onw)


def _mem_attention(mq, mem_kv, gate, onw, batch, g_blk):
    seq = mq.shape[1] // batch
    mem_len = mem_kv.shape[1] // batch
    return pl.pallas_call(
        _mem_kernel,
        grid=(batch, MEM_WIDTH // LANES),
        in_specs=[_seq_block(seq, 0), _seq_block(mem_len, 0), _seq_block(mem_len, MEM_WIDTH // LANES),
                  _seq_block(seq, g_blk), _onw_block(g_blk)],
        out_specs=_seq_block(seq, 0),
        out_shape=jax.ShapeDtypeStruct((MEM_WIDTH // LANES, batch * seq, LANES), BF16),
        compiler_params=_params("parallel", "parallel"),
        name="mem_attention",
    )(mq, mem_kv, mem_kv, gate, onw)


def kernel(x, mem, norm_w, w_in, b_forget, mem_norm_w, w_mem_kv, out_norm_w, w_out, final_norm_w):
    batch, seq, d = x.shape
    mem_len = mem.shape[1]
    depth = w_in.shape[0]
    m = batch * seq
    nblk = seq // ATT_TILE
    chunks = seq // CUM_CHUNK
    blocks = lambda width: width // LANES
    sb_q, sb_k, sb_v = 0, blocks(SB_WIDTH), 2 * blocks(SB_WIDTH)
    fx_q = 3 * blocks(SB_WIDTH)
    fx_k, fx_v = fx_q + blocks(FOX_WIDTH), fx_q + 2 * blocks(FOX_WIDTH)
    g_sb, g_fx, g_m = 0, blocks(SB_WIDTH), blocks(SB_WIDTH + FOX_WIDTH)

    ones = jnp.ones((SB_WIDTH,), F32)
    col_scale = jnp.concatenate([ones * Q_SCALE, ones, ones, ones * Q_SCALE, ones, ones])[None, :]
    off_f = QKV_WIDTH
    off_mq = off_f + FOX_HEADS
    off_g = off_mq + MEM_WIDTH

    x2d = x.reshape(m, d)
    mem2d = mem.reshape(batch * mem_len, d)
    for layer in range(depth):
        w = w_in[layer]
        wqkv = w[:, :QKV_WIDTH].astype(BF16)
        wf = jnp.pad(w[:, off_f:off_mq], ((0, 0), (0, LANES - FOX_HEADS))).astype(BF16)
        wmq = w[:, off_mq:off_g].astype(BF16)
        wg = w[:, off_g:].astype(BF16)
        qkv, mq, gate, f = _inproj(x2d, norm_w[layer][None, :], wqkv, wmq, wg, wf, col_scale)
        onw = out_norm_w[layer][None, :]

        f_rows = f[:, :FOX_HEADS].reshape(batch, seq, FOX_HEADS).transpose(0, 2, 1)
        f_rows = f_rows.reshape(batch * FOX_HEADS * chunks, CUM_CHUNK)
        bias_rows = jnp.repeat(b_forget[layer], chunks)[:, None]
        lk = _fgate(f_rows, bias_rows, batch).reshape(batch, FOX_HEADS, nblk, ATT_TILE)
        lend = lk[:, :, :, ATT_TILE - 1].reshape(batch * FOX_HEADS, nblk)

        mem_kv = _memkv(mem2d, mem_norm_w[layer][None, :], w_mem_kv[layer].astype(BF16))

        y_sb = _sb_attention(qkv, gate, onw, batch, sb_q, sb_k, sb_v, g_sb)
        y_fx = _fox_attention(lend, qkv, lk, gate, onw, batch, fx_q, fx_k, fx_v, g_fx)
        y_m = _mem_attention(mq, mem_kv, gate, onw, batch, g_m)

        x2d = _outproj(y_sb, y_fx, y_m, x2d, w_out[layer].astype(BF16),
                       final_norm_w[None, :], layer == depth - 1)
    return x2d.reshape(batch, seq, d)
```

```python
import functools

import jax
import jax.numpy as jnp
from jax import lax
from jax.experimental import pallas as pl
from jax.experimental.pallas import tpu as pltpu

HEAD_DIM = 64
SB_HEADS = 8
FOX_HEADS = 8
MEM_HEADS = 4
SB_WIDTH = SB_HEADS * HEAD_DIM
FOX_WIDTH = FOX_HEADS * HEAD_DIM
MEM_WIDTH = MEM_HEADS * HEAD_DIM
MIX_WIDTH = SB_WIDTH + FOX_WIDTH + MEM_WIDTH
QKV_WIDTH = 3 * SB_WIDTH + 3 * FOX_WIDTH
EPS = 1e-6
LOG2E = 1.4426950408889634
Q_SCALE = HEAD_DIM ** -0.5 * LOG2E
MASKED = -1e30
ZERO_WEIGHT_LOG2 = -150.0
BOUND_SLACK_LOG2 = 2.0
NORM_INFLATE = 1.0 + 2.0 ** -7

LANES = 128
ROW_TILE = 512
COL_CHUNK = 512
ATT_TILE = 256
MEM_Q_TILE = 512
CUM_CHUNK = 128
VMEM_LIMIT = 56 * 1024 * 1024

F32 = jnp.float32
BF16 = jnp.bfloat16
NT_DIMS = (((1,), (1,)), ((), ()))


def _params(*semantics):
    return pltpu.CompilerParams(dimension_semantics=semantics, vmem_limit_bytes=VMEM_LIMIT)


def _rmsnorm(x, w):
    return x * lax.rsqrt(jnp.mean(x * x, axis=-1, keepdims=True) + EPS) * w


def _dot(a, b):
    return jnp.dot(a, b, preferred_element_type=F32)


def _dot_nt(a, b):
    return lax.dot_general(a, b, NT_DIMS, preferred_element_type=F32)


def _store_lane_blocks(ref, first_block, values):
    for c in range(values.shape[1] // LANES):
        ref[first_block + c] = values[:, c * LANES:(c + 1) * LANES]


def _lane_blocks(n_blocks):
    return pl.BlockSpec((n_blocks, ROW_TILE, LANES), lambda i: (0, i, 0))


W_F = QKV_WIDTH
W_MQ = W_F + LANES
W_G = W_MQ + MEM_WIDTH
W_ALL = W_G + MIX_WIDTH


def _inproj_kernel(x_ref, nw_ref, w_ref, cs_ref, qkv_ref, mq_ref, gate_ref, f_ref):
    hb = _rmsnorm(x_ref[...], nw_ref[...]).astype(BF16)
    for c in range(0, QKV_WIDTH, COL_CHUNK):
        sl = slice(c, c + COL_CHUNK)
        _store_lane_blocks(qkv_ref, c // LANES,
                           (_dot(hb, w_ref[:, sl]) * cs_ref[:, sl]).astype(BF16))
    _store_lane_blocks(mq_ref, 0, (_dot(hb, w_ref[:, W_MQ:W_G]) * Q_SCALE).astype(BF16))
    for c in range(0, MIX_WIDTH, 256):
        _store_lane_blocks(gate_ref, c // LANES, _dot(hb, w_ref[:, W_G + c:W_G + c + 256]))
    f_ref[...] = _dot(hb, w_ref[:, W_F:W_MQ])


def _inproj(x2d, norm_w, w, col_scale):
    m, d = x2d.shape
    full = lambda shape: pl.BlockSpec(shape, lambda i: (0, 0))
    rows = lambda n: pl.BlockSpec((ROW_TILE, n), lambda i: (i, 0))
    blocks = lambda width: width // LANES
    return pl.pallas_call(
        _inproj_kernel,
        grid=(m // ROW_TILE,),
        in_specs=[rows(d), full((1, d)), full(w.shape), full((1, QKV_WIDTH))],
        out_specs=[_lane_blocks(blocks(QKV_WIDTH)), _lane_blocks(blocks(MEM_WIDTH)),
                   _lane_blocks(blocks(MIX_WIDTH)), rows(LANES)],
        out_shape=[jax.ShapeDtypeStruct((blocks(QKV_WIDTH), m, LANES), BF16),
                   jax.ShapeDtypeStruct((blocks(MEM_WIDTH), m, LANES), BF16),
                   jax.ShapeDtypeStruct((blocks(MIX_WIDTH), m, LANES), F32),
                   jax.ShapeDtypeStruct((m, LANES), F32)],
        compiler_params=_params("parallel"),
        name="inproj",
    )(x2d, norm_w, w, col_scale)


def _memkv_kernel(m_ref, nw_ref, w_ref, o_ref):
    hb = _rmsnorm(m_ref[...], nw_ref[...]).astype(BF16)
    _store_lane_blocks(o_ref, 0, _dot(hb, w_ref[...]).astype(BF16))


def _memkv(mem2d, norm_w, w):
    m, d = mem2d.shape
    n = w.shape[1]
    return pl.pallas_call(
        _memkv_kernel,
        grid=(m // ROW_TILE,),
        in_specs=[pl.BlockSpec((ROW_TILE, d), lambda i: (i, 0)),
                  pl.BlockSpec((1, d), lambda i: (0, 0)),
                  pl.BlockSpec((d, n), lambda i: (0, 0))],
        out_specs=_lane_blocks(n // LANES),
        out_shape=jax.ShapeDtypeStruct((n // LANES, m, LANES), BF16),
        compiler_params=_params("parallel"),
        name="memkv",
    )(mem2d, norm_w, w)


def _outproj_kernel(ysb_ref, yfx_ref, ym_ref, x_ref, w_ref, fnw_ref, o_ref, *, final):
    y = jnp.concatenate([ref[c] for ref in (ysb_ref, yfx_ref, ym_ref) for c in range(ref.shape[0])],
                        axis=1)
    acc = x_ref[...] + _dot(y, w_ref[...])
    if final:
        acc = _rmsnorm(acc, fnw_ref[...])
    o_ref[...] = acc


def _outproj(ysb, yfx, ym, x2d, w, final_norm_w, final):
    m, d = x2d.shape
    rows = lambda n: pl.BlockSpec((ROW_TILE, n), lambda i: (i, 0))
    return pl.pallas_call(
        functools.partial(_outproj_kernel, final=final),
        grid=(m // ROW_TILE,),
        in_specs=[_lane_blocks(ysb.shape[0]), _lane_blocks(yfx.shape[0]), _lane_blocks(ym.shape[0]),
                  rows(d), pl.BlockSpec(w.shape, lambda i: (0, 0)),
                  pl.BlockSpec((1, d), lambda i: (0, 0))],
        out_specs=rows(d),
        out_shape=jax.ShapeDtypeStruct((m, d), F32),
        compiler_params=_params("parallel"),
        name="outproj",
    )(ysb, yfx, ym, x2d, w, final_norm_w)


def _split3(x):
    hi = x.astype(BF16)
    r = x - hi.astype(F32)
    mid = r.astype(BF16)
    lo = (r - mid.astype(F32)).astype(BF16)
    return hi, mid, lo


def _fgate_kernel(f_ref, b_ref, o_ref):
    t = f_ref[...] + b_ref[...]
    lf = (jnp.minimum(t, 0.0) - jnp.log(1.0 + jnp.exp(-jnp.abs(t)))) * LOG2E
    n = lf.shape[0]
    r = lax.broadcasted_iota(jnp.int32, (CUM_CHUNK, CUM_CHUNK), 0)
    c = lax.broadcasted_iota(jnp.int32, (CUM_CHUNK, CUM_CHUNK), 1)
    upper = (r <= c).astype(BF16)
    local = sum(_dot(p, upper) for p in _split3(lf))
    totals = jnp.broadcast_to(local[:, CUM_CHUNK - 1:CUM_CHUNK], (n, CUM_CHUNK))
    chunks = n // FOX_HEADS
    gr = lax.broadcasted_iota(jnp.int32, (n, n), 0)
    gc = lax.broadcasted_iota(jnp.int32, (n, n), 1)
    earlier = ((gr // chunks == gc // chunks) & (gc < gr)).astype(BF16)
    offs = sum(_dot(earlier, p) for p in _split3(totals))
    o_ref[...] = local + offs


def _fgate(f_rows, bias_rows, batch):
    n = f_rows.shape[0] // batch
    return pl.pallas_call(
        _fgate_kernel,
        grid=(batch,),
        in_specs=[pl.BlockSpec((n, CUM_CHUNK), lambda b: (b, 0)),
                  pl.BlockSpec((n, 1), lambda b: (0, 0))],
        out_specs=pl.BlockSpec((n, CUM_CHUNK), lambda b: (b, 0)),
        out_shape=jax.ShapeDtypeStruct(f_rows.shape, F32),
        compiler_params=_params("parallel"),
        name="fgate",
    )(f_rows, bias_rows)


def _lane_ids():
    return lax.broadcasted_iota(jnp.int32, (1, LANES), 1)


def _head_lanes(lane, hh):
    return (lane >= hh * HEAD_DIM) & (lane < (hh + 1) * HEAD_DIM)


def _head_queries(q2, lane, hh):
    return jnp.where(_head_lanes(lane, hh), q2, jnp.zeros_like(q2))


def _rep(col):
    return jnp.broadcast_to(col, (col.shape[0], LANES))


def _wide(x):
    return jnp.concatenate([x] * (ATT_TILE // LANES), axis=1)


def _split2(x):
    hi = x.astype(BF16)
    return hi, (x - hi.astype(F32)).astype(BF16)


def _group_mix(lanes_in, group_in, group_out):
    r = lax.broadcasted_iota(jnp.int32, (lanes_in, LANES), 0)
    c = lax.broadcasted_iota(jnp.int32, (lanes_in, LANES), 1)
    return (r // group_in == c // group_out).astype(BF16)


def _group_sum(x, mix):
    hi, lo = _split2(x)
    return _dot(hi, mix) + _dot(lo, mix)


def _head_mix():
    return _group_mix(LANES, HEAD_DIM, HEAD_DIM)


def _finish(o, gate, onw, head_mix, denom=None):
    ms = _group_sum(o * o, head_mix) * (1.0 / HEAD_DIM)
    eps = EPS if denom is None else EPS * (denom * denom)
    yn = o * lax.rsqrt(ms + eps) * onw
    g = gate.astype(F32)
    return (yn * (g * jax.nn.sigmoid(g))).astype(BF16)


def _block_rows(i):
    start = i * ATT_TILE
    return pl.ds(start if isinstance(i, int) else pl.multiple_of(start, ATT_TILE), ATT_TILE)


def _neg_abs(x):
    bits = lax.bitcast_convert_type(x, jnp.uint32) | jnp.uint32(0x80000000)
    return lax.bitcast_convert_type(bits, F32)


def _block_masks():
    t = ATT_TILE
    row = lax.broadcasted_iota(jnp.int32, (t, t), 0)
    col = lax.broadcasted_iota(jnp.int32, (t, t), 1)
    return row, col


def _top_rows(fn, x, fresh_top):
    if not fresh_top:
        return x
    if x.shape[0] == ATT_TILE:
        return fn(x)
    return jnp.concatenate([fn(x[:ATT_TILE]), x[ATT_TILE:]], axis=0)


def _sb_kernel(q_ref, k_ref, v_ref, g_ref, onw_ref, y_ref,
               carry0_ref, carry1_ref, acc0_ref, acc1_ref, alive_ref):
    t = ATT_TILE
    nq = q_ref.shape[1] // t
    assert nq % 2 == 0
    carry_refs = (carry0_ref, carry1_ref)
    acc_refs = (acc0_ref, acc1_ref)
    lane = _lane_ids()
    row, col = _block_masks()
    strict = col < row
    incl = (row >= col).astype(BF16)
    zeros = jnp.zeros((t, LANES), F32)
    heads = range(2)

    rows_of = _block_rows

    def mask_top(u):
        return jnp.where(strict, u, 0.0)

    def qk(hh, q, j):
        return _dot_nt(_head_queries(q, lane, hh), k_ref[0, rows_of(j), :])

    def softplus_parts(z, fresh_top):
        sp = jnp.maximum(z, 0.0) + jnp.log(1.0 + jnp.exp2(-jnp.abs(z))) * LOG2E
        return (_top_rows(mask_top, sp, fresh_top).astype(BF16),)

    def suffix_sums(sp):
        return _dot(sp, incl)

    def weights(z, cs, fresh_top):
        return _top_rows(mask_top, jnp.exp2(z - cs), fresh_top).astype(BF16)

    def pv(a, j):
        return _dot(a, v_ref[0, rows_of(j), :])

    def set_alive(hh, i, carry):
        alive_ref[hh, i] = (jnp.max(carry) > ZERO_WEIGHT_LOG2).astype(jnp.int32)

    def later_key_blocks(i, _):
        def body(jj, _):
            j = i - 2 - jj
            for hh in heads:
                @pl.when(alive_ref[hh, i] != 0)
                def _():
                    z = qk(hh, q_ref[0, rows_of(i), :], j)
                    cs = suffix_sums(*softplus_parts(z, False))
                    carry_in = carry_refs[hh][rows_of(i), :]
                    carry = carry_in - _rep(cs[:, 0:1])
                    acc_refs[hh][rows_of(i), :] += pv(weights(z, cs, False), j) * jnp.exp2(carry_in)
                    carry_refs[hh][rows_of(i), :] = carry
                    set_alive(hh, i, carry)
            return 0

        lax.fori_loop(0, i - 1, body, 0)
        return 0

    for hh in heads:
        carry_refs[hh][rows_of(nq - 1), :] = zeros
        acc_refs[hh][rows_of(nq - 1), :] = zeros

    def key_block_pair(jj):
        j = nq - 1 - 2 * jj
        ib = min(j + 1, nq - 1)
        q_j = q_ref[0, rows_of(j), :]
        qa = jnp.concatenate([q_j, q_ref[0, rows_of(ib), :]], axis=0)
        qb = jnp.concatenate([q_ref[0, rows_of(j - 1), :], q_j], axis=0)
        za = [qk(hh, qa, j) for hh in heads]
        zb = [qk(hh, qb, j - 1) for hh in heads]
        pa = [softplus_parts(z, True) for z in za]
        pb = [softplus_parts(z, True) for z in zb]
        ca = [suffix_sums(*p) for p in pa]
        cb = [suffix_sums(*p) for p in pb]
        pva = [pv(weights(za[hh], ca[hh], True), j) for hh in heads]
        pvb = [pv(weights(zb[hh], cb[hh], True), j - 1) for hh in heads]
        for hh in heads:
            sums_a, sums_b = _rep(ca[hh][:, 0:1]), _rep(cb[hh][:, 0:1])
            carry_ib = carry_refs[hh][rows_of(ib), :]
            carry_j = -sums_a[:t]
            new_ib = carry_ib - sums_a[t:]
            new_j = carry_j - sums_b[t:]
            carry_refs[hh][rows_of(ib), :] = new_ib
            acc_refs[hh][rows_of(ib), :] += pva[hh][t:] * jnp.exp2(carry_ib)
            set_alive(hh, ib, new_ib)
            carry_refs[hh][rows_of(j), :] = new_j
            acc_refs[hh][rows_of(j), :] = pva[hh][:t] + pvb[hh][t:] * jnp.exp2(carry_j)
            set_alive(hh, j, new_j)
            carry_refs[hh][rows_of(j - 1), :] = -sums_b[:t]
            acc_refs[hh][rows_of(j - 1), :] = pvb[hh][:t]

    for jj in range(nq // 2):
        key_block_pair(jj)
    lax.fori_loop(2, nq, later_key_blocks, 0)

    head_mix = _head_mix()

    def finish(i, _):
        rows = rows_of(i)
        o = jnp.where(lane < HEAD_DIM, acc0_ref[rows, :], acc1_ref[rows, :])
        y_ref[0, rows, :] = _finish(o, g_ref[0, rows, :], onw_ref[...], head_mix)
        return 0

    lax.fori_loop(0, nq, finish, 0, unroll=True)


def _fox_kernel(lend_ref, q_ref, k_ref, v_ref, lk_ref, g_ref, onw_ref, y_ref,
                m0_ref, m1_ref, l0_ref, l1_ref, acc0_ref, acc1_ref, qsq_ref, bound_ref):
    t = ATT_TILE
    nq = q_ref.shape[1] // t
    assert nq % 2 == 0
    m_refs, l_refs, acc_refs = (m0_ref, m1_ref), (l0_ref, l1_ref), (acc0_ref, acc1_ref)
    hp = pl.program_id(1)
    heads = range(2)
    gate_row = [pl.program_id(0) * FOX_HEADS + 2 * hp + hh for hh in heads]
    lane = _lane_ids()
    row, col = _block_masks()
    causal = col <= row
    zeros = jnp.zeros((t, LANES), F32)
    rows_of = _block_rows

    head_sum = _group_mix(LANES, HEAD_DIM, LANES)
    head_sum = jnp.concatenate([head_sum, 1 - head_sum], axis=1)
    ksq = jnp.zeros((t, 2 * LANES), F32)
    for i in range(nq):
        qb = q_ref[0, rows_of(i), :]
        kb = k_ref[0, rows_of(i), :]
        qsq_ref[pl.ds(i, 1), :] = jnp.max(_dot(qb * qb, head_sum), axis=0, keepdims=True)
        ksq = jnp.maximum(ksq, _dot(kb * kb, head_sum))
    ksq = jnp.max(ksq, axis=0, keepdims=True)
    reach = [jnp.sqrt(qsq_ref[:, hh * LANES:hh * LANES + 1] * ksq[:, hh * LANES:hh * LANES + 1])
             * NORM_INFLATE for hh in heads]

    def qk(hh, q, j):
        return _dot_nt(_head_queries(q, lane, hh), k_ref[0, rows_of(j), :])

    def key_gates(hh, j):
        return lk_ref[0, 2 * hp + hh, pl.ds(j, 1), :]

    def halves(x):
        return x[:, :LANES] + x[:, LANES:]

    def fresh(s):
        s = jnp.where(causal, s, MASKED)
        m = jnp.max(s, axis=-1, keepdims=True)
        p = jnp.exp2(s - m)
        return _rep(m), p, halves(p)

    def update(s, m_old, l_old):
        m_new = jnp.maximum(m_old, _rep(jnp.max(s, axis=-1, keepdims=True)))
        alpha = jnp.exp2(m_old - m_new)
        p = jnp.exp2(s - _wide(m_new))
        return m_new, p, alpha * l_old + halves(p), alpha

    def pv(p, j):
        return _dot(p.astype(BF16), v_ref[0, rows_of(j), :])

    def set_bound(hh, i, m_diag):
        gap = reach[hh][i:i + 1, :] - jnp.min(m_diag, axis=0, keepdims=True)[:, 0:1]
        bound_ref[hh, i] = jnp.max(gap) + lend_ref[gate_row[hh], i] + BOUND_SLACK_LOG2

    def later_key_blocks(i, _):
        def body(jj, _):
            j = i - 2 - jj
            for hh in heads:
                @pl.when(bound_ref[hh, i] - lend_ref[gate_row[hh], j] >= ZERO_WEIGHT_LOG2)
                def _():
                    s = qk(hh, q_ref[0, rows_of(i), :], j)
                    s = s + (lend_ref[gate_row[hh], i] - key_gates(hh, j))
                    m, p, l, alpha = update(s, m_refs[hh][rows_of(i), :], l_refs[hh][rows_of(i), :])
                    m_refs[hh][rows_of(i), :] = m
                    l_refs[hh][rows_of(i), :] = l
                    acc_refs[hh][rows_of(i), :] = alpha * acc_refs[hh][rows_of(i), :] + pv(p, j)
            return 0

        lax.fori_loop(0, i - 1, body, 0)
        return 0

    for hh in heads:
        for ref in (m_refs[hh], l_refs[hh], acc_refs[hh]):
            ref[rows_of(nq - 1), :] = zeros

    def key_block_pair(jj):
        j = nq - 1 - 2 * jj
        ib = min(j + 1, nq - 1)
        q_j = q_ref[0, rows_of(j), :]
        qa = jnp.concatenate([q_j, q_ref[0, rows_of(ib), :]], axis=0)
        qb = jnp.concatenate([q_ref[0, rows_of(j - 1), :], q_j], axis=0)
        za = [qk(hh, qa, j) for hh in heads]
        zb = [qk(hh, qb, j - 1) for hh in heads]
        wa, wb, new = [], [], []
        for hh in heads:
            g = gate_row[hh]
            gates_a, gates_b = key_gates(hh, j), key_gates(hh, j - 1)
            m_a, p_a, l_a = fresh(za[hh][:t] + (lend_ref[g, j] - gates_a))
            m_ib, p_ib, l_ib, alpha_ib = update(za[hh][t:] + (lend_ref[g, ib] - gates_a),
                                                m_refs[hh][rows_of(ib), :], l_refs[hh][rows_of(ib), :])
            m_b, p_b, l_b = fresh(zb[hh][:t] + (lend_ref[g, j - 1] - gates_b))
            m_j, p_j, l_j, alpha_j = update(zb[hh][t:] + (lend_ref[g, j] - gates_b), m_a, l_a)
            wa.append(jnp.concatenate([p_a, p_ib], axis=0))
            wb.append(jnp.concatenate([p_b, p_j], axis=0))
            new.append((m_a, m_ib, l_ib, alpha_ib, m_b, l_b, m_j, l_j, alpha_j))
        pva = [pv(w, j) for w in wa]
        pvb = [pv(w, j - 1) for w in wb]
        for hh in heads:
            m_a, m_ib, l_ib, alpha_ib, m_b, l_b, m_j, l_j, alpha_j = new[hh]
            m_refs[hh][rows_of(ib), :] = m_ib
            l_refs[hh][rows_of(ib), :] = l_ib
            acc_refs[hh][rows_of(ib), :] = alpha_ib * acc_refs[hh][rows_of(ib), :] + pva[hh][t:]
            m_refs[hh][rows_of(j), :] = m_j
            l_refs[hh][rows_of(j), :] = l_j
            acc_refs[hh][rows_of(j), :] = alpha_j * pva[hh][:t] + pvb[hh][t:]
            set_bound(hh, j, m_a)
            m_refs[hh][rows_of(j - 1), :] = m_b
            l_refs[hh][rows_of(j - 1), :] = l_b
            acc_refs[hh][rows_of(j - 1), :] = pvb[hh][:t]
            set_bound(hh, j - 1, m_b)

    for jj in range(nq // 2):
        key_block_pair(jj)
    lax.fori_loop(2, nq, later_key_blocks, 0)

    head_mix = _head_mix()
    pair_mix = _group_mix(2 * LANES, LANES, HEAD_DIM)

    def finish(i, _):
        rows = rows_of(i)
        partial = jnp.concatenate([l0_ref[rows, :], l1_ref[rows, :]], axis=1)
        acc = jnp.where(lane < HEAD_DIM, acc0_ref[rows, :], acc1_ref[rows, :])
        denom = _dot(partial.astype(BF16), pair_mix)
        y_ref[0, rows, :] = _finish(acc, g_ref[0, rows, :], onw_ref[...], head_mix, denom)
        return 0

    lax.fori_loop(0, nq, finish, 0, unroll=True)


def _mem_kernel(q_ref, k_ref, v_ref, g_ref, onw_ref, y_ref):
    k2 = k_ref[0]
    v2 = v_ref[0]
    lane = _lane_ids()
    first = lane < HEAD_DIM
    head_mix = _head_mix()
    tiles = [pl.ds(r, MEM_Q_TILE) for r in range(0, q_ref.shape[1], MEM_Q_TILE)]
    chains = [(rows, hh) for rows in tiles for hh in range(2)]
    scores = [_dot_nt(_head_queries(q_ref[0, rows, :], lane, hh), k2) for rows, hh in chains]
    probs = [jnp.exp2(s - jnp.max(s, axis=-1, keepdims=True)) for s in scores]
    sums = [jnp.sum(p, axis=-1, keepdims=True) for p in probs]
    outs = [_dot(p.astype(BF16), v2) for p in probs]
    for n, rows in enumerate(tiles):
        o = jnp.where(first, outs[2 * n], outs[2 * n + 1])
        denom = jnp.where(first, sums[2 * n], sums[2 * n + 1])
        y_ref[0, rows, :] = _finish(o, g_ref[0, rows, :], onw_ref[...], head_mix, denom)


def _seq_block(seq, first_block):
    return pl.BlockSpec((1, seq, LANES), lambda b, hp, *_: (first_block + hp, b, 0))


def _onw_block(first_block):
    return pl.BlockSpec((1, LANES), lambda b, hp, *_: (0, first_block + hp))


def _sb_attention(qkv, gate, onw, batch, q_blk, k_blk, v_blk, g_blk):
    seq = qkv.shape[1] // batch
    nq = seq // ATT_TILE
    state = pltpu.VMEM((seq, LANES), F32)
    return pl.pallas_call(
        _sb_kernel,
        grid=(batch, SB_WIDTH // LANES),
        in_specs=[_seq_block(seq, q_blk), _seq_block(seq, k_blk), _seq_block(seq, v_blk),
                  _seq_block(seq, g_blk), _onw_block(g_blk)],
        out_specs=_seq_block(seq, 0),
        out_shape=jax.ShapeDtypeStruct((SB_WIDTH // LANES, batch * seq, LANES), BF16),
        scratch_shapes=[state, state, state, state, pltpu.SMEM((2, nq), jnp.int32)],
        compiler_params=_params("parallel", "parallel"),
        name="sb_attention",
    )(qkv, qkv, qkv, gate, onw)


def _fox_attention(lend, qkv, lk, gate, onw, batch, q_blk, k_blk, v_blk, g_blk):
    seq = qkv.shape[1] // batch
    nq = seq // ATT_TILE
    state = pltpu.VMEM((seq, LANES), F32)
    grid_spec = pltpu.PrefetchScalarGridSpec(
        num_scalar_prefetch=1,
        grid=(batch, FOX_WIDTH // LANES),
        in_specs=[_seq_block(seq, q_blk), _seq_block(seq, k_blk), _seq_block(seq, v_blk),
                  pl.BlockSpec((1, FOX_HEADS, nq, ATT_TILE), lambda b, hp, *_: (b, 0, 0, 0)),
                  _seq_block(seq, g_blk), _onw_block(g_blk)],
        out_specs=_seq_block(seq, 0),
        scratch_shapes=[state] * 6 + [pltpu.VMEM((nq, 2 * LANES), F32), pltpu.SMEM((2, nq), F32)],
    )
    return pl.pallas_call(
        _fox_kernel,
        grid_spec=grid_spec,
        out_shape=jax.ShapeDtypeStruct((FOX_WIDTH // LANES, batch * seq, LANES), BF16),
        compiler_params=_params("parallel", "parallel"),
        name="fox_attention",
    )(lend, qkv, qkv, qkv, lk, gate, onw)


def _mem_attention(mq, mem_kv, gate, onw, batch, g_blk):
    seq = mq.shape[1] // batch
    mem_len = mem_kv.shape[1] // batch
    return pl.pallas_call(
        _mem_kernel,
        grid=(batch, MEM_WIDTH // LANES),
        in_specs=[_seq_block(seq, 0), _seq_block(mem_len, 0), _seq_block(mem_len, MEM_WIDTH // LANES),
                  _seq_block(seq, g_blk), _onw_block(g_blk)],
        out_specs=_seq_block(seq, 0),
        out_shape=jax.ShapeDtypeStruct((MEM_WIDTH // LANES, batch * seq, LANES), BF16),
        compiler_params=_params("parallel", "parallel"),
        name="mem_attention",
    )(mq, mem_kv, mem_kv, gate, onw)


def kernel(x, mem, norm_w, w_in, b_forget, mem_norm_w, w_mem_kv, out_norm_w, w_out, final_norm_w):
    batch, seq, d = x.shape
    mem_len = mem.shape[1]
    depth = w_in.shape[0]
    m = batch * seq
    nblk = seq // ATT_TILE
    chunks = seq // CUM_CHUNK
    blocks = lambda width: width // LANES
    sb_q, sb_k, sb_v = 0, blocks(SB_WIDTH), 2 * blocks(SB_WIDTH)
    fx_q = 3 * blocks(SB_WIDTH)
    fx_k, fx_v = fx_q + blocks(FOX_WIDTH), fx_q + 2 * blocks(FOX_WIDTH)
    g_sb, g_fx, g_m = 0, blocks(SB_WIDTH), blocks(SB_WIDTH + FOX_WIDTH)

    ones = jnp.ones((SB_WIDTH,), F32)
    col_scale = jnp.concatenate([ones * Q_SCALE, ones, ones, ones * Q_SCALE, ones, ones])[None, :]
    gate_pad = jnp.zeros(w_in.shape[:2] + (LANES - FOX_HEADS,), w_in.dtype)
    w_in_b = jnp.concatenate([w_in[:, :, :QKV_WIDTH + FOX_HEADS], gate_pad,
                              w_in[:, :, QKV_WIDTH + FOX_HEADS:]], axis=2).astype(BF16)
    w_out_b = w_out.astype(BF16)
    w_mem_b = w_mem_kv.astype(BF16)

    x2d = x.reshape(m, d)
    mem2d = mem.reshape(batch * mem_len, d)
    for layer in range(depth):
        qkv, mq, gate, f = _inproj(x2d, norm_w[layer][None, :], w_in_b[layer], col_scale)
        onw = out_norm_w[layer][None, :]

        f_rows = f[:, :FOX_HEADS].reshape(batch, seq, FOX_HEADS).transpose(0, 2, 1)
        f_rows = f_rows.reshape(batch * FOX_HEADS * chunks, CUM_CHUNK)
        bias_rows = jnp.repeat(b_forget[layer], chunks)[:, None]
        lk = _fgate(f_rows, bias_rows, batch).reshape(batch, FOX_HEADS, nblk, ATT_TILE)
        lend = lk[:, :, :, ATT_TILE - 1].reshape(batch * FOX_HEADS, nblk)

        mem_kv = _memkv(mem2d, mem_norm_w[layer][None, :], w_mem_b[layer])

        y_sb = _sb_attention(qkv, gate, onw, batch, sb_q, sb_k, sb_v, g_sb)
        y_fx = _fox_attention(lend, qkv, lk, gate, onw, batch, fx_q, fx_k, fx_v, g_fx)
        y_m = _mem_attention(mq, mem_kv, gate, onw, batch, g_m)

        x2d = _outproj(y_sb, y_fx, y_m, x2d, w_out_b[layer],
                       final_norm_w[None, :], layer == depth - 1)
    return x2d.reshape(batch, seq, d)
```
